```python
import jax, jax.numpy as jnp
from jax import lax
import numpy as np

D_MODEL = 2048
BATCH = 2
SEQ = 4096
DEPTH = 4
DEC_BATCH = 32
DEC_SEQ = 4
PAST_LEN = 16384
PAGE_SIZE = 128

HEAD_DIM = 64
N_Q_HEADS = 16
N_KV_HEADS = 4
GQA_GROUP = N_Q_HEADS // N_KV_HEADS
ATTN_WIDTH = N_Q_HEADS * HEAD_DIM
KV_WIDTH = N_KV_HEADS * HEAD_DIM
WINDOW = 128
BLOCK = WINDOW
POOL_WINDOWS = (2, 4, 8, 16)
N_POOL_GROUPS = len(POOL_WINDOWS)
POOL_GROUP_WIDTH = 256
POOL_WIDTH = N_POOL_GROUPS * POOL_GROUP_WIDTH
POOL_STATE = max(POOL_WINDOWS) - 1
MIX_WIDTH = ATTN_WIDTH + POOL_WIDTH
IN_WIDTH = ATTN_WIDTH + 2 * KV_WIDTH + POOL_WIDTH
D_FF = 4 * D_MODEL
EPS = 1e-6

kernel_name = 'hymba_pool_swa_sink_decoder'


def rmsnorm(x, g):
    xf = x.astype(jnp.float32)
    y = xf * lax.rsqrt(jnp.mean(xf * xf, axis=-1, keepdims=True) + EPS)
    return (y * g.astype(jnp.float32)).astype(x.dtype)


def alibi_slopes():
    h = jnp.arange(1, N_Q_HEADS + 1, dtype=jnp.float32)
    return jnp.exp2(-8.0 * h / N_Q_HEADS).reshape(N_KV_HEADS, GQA_GROUP)


def window_attention(q, k, v, qpos, kpos, sinks):
    lead = q.shape[:-2]
    qf = q.astype(jnp.float32).reshape(lead + (N_KV_HEADS, GQA_GROUP, HEAD_DIM))
    s = jnp.einsum('...qkgd,...skd->...kgqs', qf, k.astype(jnp.float32)) * (HEAD_DIM ** -0.5)
    dist = qpos[..., :, None] - kpos[..., None, :]
    valid = (dist >= 0) & (dist < WINDOW) & (kpos[..., None, :] >= 0)
    bias = -alibi_slopes()[:, :, None, None] * dist.astype(jnp.float32)[..., None, None, :, :]
    s = jnp.where(valid[..., None, None, :, :], s + bias, -jnp.inf)
    sink = sinks.astype(jnp.float32).reshape(N_KV_HEADS, GQA_GROUP)[:, :, None, None]
    m = jnp.maximum(jnp.max(s, axis=-1, keepdims=True), sink)
    p = jnp.exp(s - m)
    denom = jnp.sum(p, axis=-1, keepdims=True) + jnp.exp(sink - m)
    o = jnp.einsum('...kgqs,...skd->...qkgd', p / denom, v.astype(jnp.float32))
    return o.reshape(lead + (ATTN_WIDTH,)).astype(q.dtype)


def shift_blocks(b):
    return jnp.pad(b, ((0, 0), (1, 0), (0, 0), (0, 0), (0, 0)))[:, :-1]


def pool_mix(u, buf, pos0, w_pool, scale):
    n, t, _ = u.shape
    p = jnp.concatenate([buf, u], axis=1)
    pf = p.astype(jnp.float32)
    cs0 = jnp.pad(jnp.cumsum(pf, axis=1), ((0, 0), (1, 0), (0, 0)))
    uf = pf[:, POOL_STATE:]
    pos = pos0 + jnp.arange(t)
    groups = []
    for gi, w in enumerate(POOL_WINDOWS):
        lo, hi = gi * POOL_GROUP_WIDTH, (gi + 1) * POOL_GROUP_WIDTH
        upper = cs0[:, POOL_STATE + 1:POOL_STATE + 1 + t, lo:hi]
        lower = cs0[:, POOL_STATE + 1 - w:POOL_STATE + 1 - w + t, lo:hi]
        cnt = jnp.minimum(pos + 1, w).astype(jnp.float32)[:, None]
        groups.append((upper - lower) / cnt - uf[..., lo:hi])
    z = jnp.stack(groups, axis=2)
    z = jnp.einsum('ntgc,gcd->ntgd', z, w_pool.astype(jnp.float32))
    out = z.reshape(n, t, POOL_WIDTH) * scale.astype(jnp.float32)
    return out.astype(u.dtype), p[:, -POOL_STATE:]


def mixer_sublayer(x, k_buf, v_buf, pool_buf, pos0, w_in, w_out, w_pool, pool_scale, sinks, g_pre, g_post):
    n, t, _ = x.shape
    h = rmsnorm(x, g_pre)
    proj = jnp.einsum('ntd,de->nte', h, w_in)
    q = proj[..., :ATTN_WIDTH].reshape(n, t, N_Q_HEADS, HEAD_DIM)
    k = proj[..., ATTN_WIDTH:ATTN_WIDTH + KV_WIDTH].reshape(n, t, N_KV_HEADS, HEAD_DIM)
    v = proj[..., ATTN_WIDTH + KV_WIDTH:ATTN_WIDTH + 2 * KV_WIDTH].reshape(n, t, N_KV_HEADS, HEAD_DIM)
    u = proj[..., ATTN_WIDTH + 2 * KV_WIDTH:]
    if k_buf is None:
        nb = t // BLOCK
        qb = q.reshape(n, nb, BLOCK, N_Q_HEADS, HEAD_DIM)
        kb = k.reshape(n, nb, BLOCK, N_KV_HEADS, HEAD_DIM)
        vb = v.reshape(n, nb, BLOCK, N_KV_HEADS, HEAD_DIM)
        kk = jnp.concatenate([shift_blocks(kb), kb], axis=2)
        vv = jnp.concatenate([shift_blocks(vb), vb], axis=2)
        qpos = (pos0 + jnp.arange(t)).reshape(nb, BLOCK)
        kpos = qpos[:, :1] - BLOCK + jnp.arange(2 * BLOCK)[None, :]
        a = window_attention(qb, kk, vv, qpos, kpos, sinks).reshape(n, t, ATTN_WIDTH)
        new_k, new_v = k[:, -WINDOW:], v[:, -WINDOW:]
        pool_buf = jnp.zeros((n, POOL_STATE, POOL_WIDTH), u.dtype)
    else:
        kk = jnp.concatenate([k_buf.astype(k.dtype), k], axis=1)
        vv = jnp.concatenate([v_buf.astype(v.dtype), v], axis=1)
        qpos = pos0 + jnp.arange(t)
        kpos = pos0 - WINDOW + jnp.arange(WINDOW + t)
        a = window_attention(q, kk, vv, qpos, kpos, sinks)
        new_k, new_v = kk[:, -WINDOW:], vv[:, -WINDOW:]
    pl, new_pool = pool_mix(u, pool_buf.astype(u.dtype), pos0, w_pool, pool_scale)
    out = jnp.einsum('nte,ed->ntd', jnp.concatenate([a, pl], axis=-1), w_out)
    return x + rmsnorm(out, g_post), new_k, new_v, new_pool


def ffn_sublayer(x, w_up, w_down, g_pre, g_post):
    h = rmsnorm(x, g_pre)
    a = jnp.square(jax.nn.relu(jnp.einsum('ntd,df->ntf', h, w_up)))
    return x + rmsnorm(jnp.einsum('ntf,fd->ntd', a, w_down), g_post)


def setup_inputs(seed: int = 0) -> dict:
    key = jax.random.key(seed)
    ks = jax.random.split(key, 16)
    f32 = jnp.float32
    nrm = lambda k, s: jax.random.normal(k, s, f32)
    return {
        'x_prompt': nrm(ks[0], (BATCH, SEQ, D_MODEL)),
        'x_sample': nrm(ks[1], (DEC_BATCH, DEC_SEQ, D_MODEL)),
        'cache_k': nrm(ks[2], (DEPTH, DEC_BATCH, WINDOW, N_KV_HEADS, HEAD_DIM)),
        'cache_v': nrm(ks[3], (DEPTH, DEC_BATCH, WINDOW, N_KV_HEADS, HEAD_DIM)),
        'state_pool': nrm(ks[4], (DEPTH, DEC_BATCH, POOL_STATE, POOL_WIDTH)),
        'w_in': nrm(ks[5], (DEPTH, D_MODEL, IN_WIDTH)) * D_MODEL ** -0.5,
        'w_out': nrm(ks[6], (DEPTH, MIX_WIDTH, D_MODEL)) * MIX_WIDTH ** -0.5,
        'w_pool': nrm(ks[7], (DEPTH, N_POOL_GROUPS, POOL_GROUP_WIDTH, POOL_GROUP_WIDTH)) * POOL_GROUP_WIDTH ** -0.5,
        'pool_scale': 1.0 + 0.1 * nrm(ks[8], (DEPTH, POOL_WIDTH)),
        'attn_sinks': 0.5 * nrm(ks[9], (DEPTH, N_Q_HEADS)),
        'g_pre_mix': 1.0 + 0.05 * nrm(ks[10], (DEPTH, D_MODEL)),
        'g_post_mix': 1.0 + 0.05 * nrm(ks[11], (DEPTH, D_MODEL)),
        'g_pre_ffn': 1.0 + 0.05 * nrm(ks[12], (DEPTH, D_MODEL)),
        'g_post_ffn': 1.0 + 0.05 * nrm(ks[13], (DEPTH, D_MODEL)),
        'w_up': nrm(ks[14], (DEPTH, D_MODEL, D_FF)) * D_MODEL ** -0.5,
        'w_down': nrm(ks[15], (DEPTH, D_FF, D_MODEL)) * D_FF ** -0.5,
    }


def reference(x_prompt, x_sample, cache_k, cache_v, state_pool, w_in, w_out, w_pool, pool_scale, attn_sinks,
              g_pre_mix, g_post_mix, g_pre_ffn, g_post_ffn, w_up, w_down):
    xp, xs = x_prompt, x_sample
    kp_l, vp_l, pp_l, ks_l, vs_l, ps_l = [], [], [], [], [], []
    for l in range(DEPTH):
        xp, kp, vp, pp = mixer_sublayer(xp, None, None, None, 0, w_in[l], w_out[l], w_pool[l], pool_scale[l],
                                        attn_sinks[l], g_pre_mix[l], g_post_mix[l])
        xs, kn, vn, pn = mixer_sublayer(xs, cache_k[l], cache_v[l], state_pool[l], PAST_LEN, w_in[l], w_out[l],
                                        w_pool[l], pool_scale[l], attn_sinks[l], g_pre_mix[l], g_post_mix[l])
        xp = ffn_sublayer(xp, w_up[l], w_down[l], g_pre_ffn[l], g_post_ffn[l])
        xs = ffn_sublayer(xs, w_up[l], w_down[l], g_pre_ffn[l], g_post_ffn[l])
        kp_l.append(kp); vp_l.append(vp); pp_l.append(pp)
        ks_l.append(kn); vs_l.append(vn); ps_l.append(pn)
    k_prompt = jnp.stack(kp_l); v_prompt = jnp.stack(vp_l); pool_prompt = jnp.stack(pp_l)
    k_sample = jnp.stack(ks_l); v_sample = jnp.stack(vs_l); pool_sample = jnp.stack(ps_l)
    return (xp, xs, k_prompt, v_prompt, pool_prompt, k_sample, v_sample, pool_sample)
```

```python
import functools

import jax
import jax.numpy as jnp
import numpy as np
from jax import lax
from jax.experimental import pallas as pl
from jax.experimental.pallas import tpu as pltpu

D_MODEL = 2048
DEPTH = 4
PAST_LEN = 16384
HEAD_DIM = 64
N_Q_HEADS = 16
N_KV_HEADS = 4
ATTN_WIDTH = N_Q_HEADS * HEAD_DIM
KV_WIDTH = N_KV_HEADS * HEAD_DIM
WINDOW = 128
POOL_WINDOWS = (2, 4, 8, 16)
POOL_GROUP_WIDTH = 256
POOL_WIDTH = len(POOL_WINDOWS) * POOL_GROUP_WIDTH
POOL_STATE = max(POOL_WINDOWS) - 1
POOL_PREV_ROWS = POOL_STATE + 1
MIX_WIDTH = ATTN_WIDTH + POOL_WIDTH
IN_WIDTH = ATTN_WIDTH + 2 * KV_WIDTH + POOL_WIDTH
D_FF = 4 * D_MODEL
EPS = 1e-6

LANES = 128
SUBLANES = 8
NEG_BIG = -1e30
VMEM_LIMIT_BYTES = 56 * 1024 * 1024
SAMPLE_ROWS = SUBLANES

BF16 = jnp.bfloat16
F32 = jnp.float32


def _rms(x, g):
    ms = jnp.mean(x * x, axis=-1, keepdims=True)
    return (x * lax.rsqrt(ms + EPS)) * g


def _params(semantics):
    return pltpu.CompilerParams(dimension_semantics=semantics, vmem_limit_bytes=VMEM_LIMIT_BYTES)


def _inproj_kernel(x_ref, g_ref, w_ref, q_ref, kv_ref, u_ref):
    h = _rms(x_ref[...], g_ref[...]).astype(BF16)
    q = jnp.dot(h, w_ref[:, :ATTN_WIDTH], preferred_element_type=F32)
    q_ref[...] = q * (HEAD_DIM ** -0.5)
    kv_ref[...] = jnp.dot(h, w_ref[:, ATTN_WIDTH:ATTN_WIDTH + 2 * KV_WIDTH], preferred_element_type=F32)
    u_ref[...] = jnp.dot(h, w_ref[:, ATTN_WIDTH + 2 * KV_WIDTH:], preferred_element_type=F32)


def _inproj(x, g, w, tm):
    m = x.shape[0]
    return pl.pallas_call(
        _inproj_kernel,
        grid=(m // tm,),
        in_specs=[
            pl.BlockSpec((tm, D_MODEL), lambda i: (i, 0)),
            pl.BlockSpec((1, D_MODEL), lambda i: (0, 0)),
            pl.BlockSpec((D_MODEL, IN_WIDTH), lambda i: (0, 0)),
        ],
        out_specs=[
            pl.BlockSpec((tm, ATTN_WIDTH), lambda i: (i, 0)),
            pl.BlockSpec((tm, 2 * KV_WIDTH), lambda i: (i, 0)),
            pl.BlockSpec((tm, POOL_WIDTH), lambda i: (i, 0)),
        ],
        out_shape=[
            jax.ShapeDtypeStruct((m, ATTN_WIDTH), F32),
            jax.ShapeDtypeStruct((m, 2 * KV_WIDTH), F32),
            jax.ShapeDtypeStruct((m, POOL_WIDTH), F32),
        ],
        compiler_params=_params(("arbitrary",)),
        name="inproj",
    )(x, g, w)


def _half_padded(x2, head_in_high_half):
    lane = lax.broadcasted_iota(jnp.int32, x2.shape, 1)
    swapped = pltpu.roll(x2, HEAD_DIM, axis=1)
    in_lo, in_hi = (swapped, x2) if head_in_high_half else (x2, swapped)
    lo = jnp.where(lane < HEAD_DIM, in_lo, 0.0).astype(BF16)
    hi = jnp.where(lane >= HEAD_DIM, in_hi, 0.0).astype(BF16)
    return lo, hi


def _attention(rows, q, kk, vv, bias_at, sinks_ref, out_ref):
    row = lax.broadcasted_iota(jnp.int32, (2 * rows, 1), 0)
    lane = lax.broadcasted_iota(jnp.int32, (2 * rows, LANES), 1)
    for kh in range(N_KV_HEADS):
        col = (kh // 2) * LANES
        k_lo, k_hi = _half_padded(kk[:, col:col + LANES], kh % 2 == 1)
        v_lo, v_hi = _half_padded(vv[:, col:col + LANES], kh % 2 == 1)
        c0 = 2 * kh * LANES
        qq = jnp.concatenate([q[:, c0:c0 + LANES], q[:, c0 + LANES:c0 + 2 * LANES]], axis=0).astype(BF16)
        probs, recips = [], []
        for par, k_pad in enumerate((k_lo, k_hi)):
            s = lax.dot_general(qq, k_pad, (((1,), (1,)), ((), ())), preferred_element_type=F32)
            s = s + bias_at(2 * kh + par)
            sink = jnp.where(row < rows, sinks_ref[4 * kh + par], sinks_ref[4 * kh + 2 + par])
            m = jnp.maximum(jnp.max(s, axis=-1, keepdims=True), sink)
            p = jnp.exp(s - m)
            denom = jnp.sum(p, axis=-1, keepdims=True) + jnp.exp(sink - m)
            probs.append(p.astype(BF16))
            recips.append(1.0 / denom)
        o = (jnp.dot(probs[0], v_lo, preferred_element_type=F32)
             + jnp.dot(probs[1], v_hi, preferred_element_type=F32))
        o = o * jnp.where(lane < HEAD_DIM, recips[0], recips[1])
        out_ref[:, c0:c0 + LANES] = o[:rows].astype(out_ref.dtype)
        out_ref[:, c0 + LANES:c0 + 2 * LANES] = o[rows:].astype(out_ref.dtype)


def _pool(rows, u_prev, u_cur, pos, wpool_ref, pscale_ref, out_ref):
    ext = jnp.concatenate([u_prev, u_cur], axis=0)
    for gi, w in enumerate(POOL_WINDOWS):
        lo = gi * POOL_GROUP_WIDTH
        e = ext[:, lo:lo + POOL_GROUP_WIDTH]
        s, d = e, 1
        while d < w:
            s = s + pltpu.roll(s, d, axis=0)
            d *= 2
        cnt = jnp.minimum(pos + 1, w).astype(F32)
        z = s[POOL_PREV_ROWS:] / cnt - e[POOL_PREV_ROWS:]
        zz = jnp.dot(z.astype(BF16), wpool_ref[gi], preferred_element_type=F32)
        zz = zz * pscale_ref[:, lo:lo + POOL_GROUP_WIDTH]
        out_ref[:, ATTN_WIDTH + lo:ATTN_WIDTH + lo + POOL_GROUP_WIDTH] = zz.astype(out_ref.dtype)


def _mixer_prompt_kernel(sinks_ref, q_ref, kvp_ref, kvc_ref, up_ref, uc_ref, bias_ref, wpool_ref, pscale_ref,
                         out_ref):
    b = pl.program_id(1)
    first = (b == 0).astype(jnp.int32)
    kvp, kvc = kvp_ref[...], kvc_ref[...]
    kk = jnp.concatenate([kvp[:, :KV_WIDTH], kvc[:, :KV_WIDTH]], axis=0)
    vv = jnp.concatenate([kvp[:, KV_WIDTH:], kvc[:, KV_WIDTH:]], axis=0)
    _attention(WINDOW, q_ref[...], kk, vv, lambda i: bias_ref[first, i], sinks_ref, out_ref)
    u_prev = jnp.where(b == 0, 0.0, up_ref[...])
    pos = b * WINDOW + lax.broadcasted_iota(jnp.int32, (WINDOW, 1), 0)
    _pool(WINDOW, u_prev, uc_ref[...], pos, wpool_ref, pscale_ref, out_ref)


def _mixer_prompt(q, kv, u, sinks, bias, wpool, pscale, n_seq, n_blk):
    m = q.shape[0]
    sub = WINDOW // POOL_PREV_ROWS

    def cur(n, b):
        return (n * n_blk + b, 0)

    def prev(n, b):
        return (n * n_blk + jnp.maximum(b - 1, 0), 0)

    def prev_u(n, b):
        return (jnp.maximum((n * n_blk + b) * sub - 1, 0), 0)

    return pl.pallas_call(
        _mixer_prompt_kernel,
        grid=(n_seq, n_blk),
        in_specs=[
            pl.BlockSpec(memory_space=pltpu.SMEM),
            pl.BlockSpec((WINDOW, ATTN_WIDTH), cur),
            pl.BlockSpec((WINDOW, 2 * KV_WIDTH), prev),
            pl.BlockSpec((WINDOW, 2 * KV_WIDTH), cur),
            pl.BlockSpec((POOL_PREV_ROWS, POOL_WIDTH), prev_u),
            pl.BlockSpec((WINDOW, POOL_WIDTH), cur),
            pl.BlockSpec(bias.shape, lambda n, b: (0, 0, 0, 0)),
            pl.BlockSpec(wpool.shape, lambda n, b: (0, 0, 0)),
            pl.BlockSpec((1, POOL_WIDTH), lambda n, b: (0, 0)),
        ],
        out_specs=pl.BlockSpec((WINDOW, MIX_WIDTH), cur),
        out_shape=jax.ShapeDtypeStruct((m, MIX_WIDTH), BF16),
        compiler_params=_params(("arbitrary", "arbitrary")),
        name="mixer_prompt",
    )(sinks, q, kv, kv, u, u, bias, wpool, pscale)


def _shift_in(old, new, n_new):
    r = old.shape[0]
    rolled = pltpu.roll(old, r - n_new, axis=0)
    tail = pltpu.roll(new, SUBLANES - n_new, axis=0)
    row = lax.broadcasted_iota(jnp.int32, (SUBLANES, old.shape[1]), 0)
    last = jnp.where(row < SUBLANES - n_new, rolled[r - SUBLANES:], tail)
    return jnp.concatenate([rolled[:r - SUBLANES], last], axis=0)


def _mixer_sample_kernel(n_new, sinks_ref, q_ref, kv_ref, ck_ref, cv_ref, st_ref, u_ref, bias_ref, wpool_ref,
                         pscale_ref, out_ref, ko_ref, vo_ref, po_ref):
    rows = SAMPLE_ROWS
    kv = kv_ref[...]
    ck, cv = ck_ref[...], cv_ref[...]
    pad = jnp.zeros((WINDOW - rows, KV_WIDTH), F32)
    kk = jnp.concatenate([ck, kv[:, :KV_WIDTH], pad], axis=0)
    vv = jnp.concatenate([cv, kv[:, KV_WIDTH:], pad], axis=0)
    _attention(rows, q_ref[...], kk, vv, lambda i: bias_ref[i], sinks_ref, out_ref)
    u_prev, u_cur = st_ref[...], u_ref[...]
    pos = PAST_LEN + lax.broadcasted_iota(jnp.int32, (rows, 1), 0)
    _pool(rows, u_prev, u_cur, pos, wpool_ref, pscale_ref, out_ref)
    ko_ref[...] = _shift_in(ck, kv[:, :KV_WIDTH], n_new)
    vo_ref[...] = _shift_in(cv, kv[:, KV_WIDTH:], n_new)
    hist = _shift_in(u_prev, u_cur, n_new)
    po_ref[...] = pltpu.roll(hist, POOL_STATE, axis=0)[:POOL_STATE]


def _mixer_sample(q, kv, u, cache_k, cache_v, state, sinks, bias, wpool, pscale, n_new):
    n_seq = cache_k.shape[0]
    rows = SAMPLE_ROWS
    seq3 = lambda width: pl.BlockSpec((None, rows, width), lambda n: (n, 0, 0))
    return pl.pallas_call(
        functools.partial(_mixer_sample_kernel, n_new),
        grid=(n_seq,),
        in_specs=[
            pl.BlockSpec(memory_space=pltpu.SMEM),
            seq3(ATTN_WIDTH),
            seq3(2 * KV_WIDTH),
            pl.BlockSpec((None, WINDOW, KV_WIDTH), lambda n: (n, 0, 0)),
            pl.BlockSpec((None, WINDOW, KV_WIDTH), lambda n: (n, 0, 0)),
            pl.BlockSpec((None, POOL_PREV_ROWS, POOL_WIDTH), lambda n: (n, 0, 0)),
            seq3(POOL_WIDTH),
            pl.BlockSpec(bias.shape, lambda n: (0, 0, 0)),
            pl.BlockSpec(wpool.shape, lambda n: (0, 0, 0)),
            pl.BlockSpec((1, POOL_WIDTH), lambda n: (0, 0)),
        ],
        out_specs=[
            seq3(MIX_WIDTH),
            pl.BlockSpec((None, WINDOW, KV_WIDTH), lambda n: (n, 0, 0)),
            pl.BlockSpec((None, WINDOW, KV_WIDTH), lambda n: (n, 0, 0)),
            pl.BlockSpec((None, POOL_STATE, POOL_WIDTH), lambda n: (n, 0, 0)),
        ],
        out_shape=[
            jax.ShapeDtypeStruct((n_seq, rows, MIX_WIDTH), F32),
            jax.ShapeDtypeStruct((n_seq, WINDOW, KV_WIDTH), F32),
            jax.ShapeDtypeStruct((n_seq, WINDOW, KV_WIDTH), F32),
            jax.ShapeDtypeStruct((n_seq, POOL_STATE, POOL_WIDTH), F32),
        ],
        compiler_params=_params(("arbitrary",)),
        name="mixer_sample",
    )(sinks, q.reshape(n_seq, rows, ATTN_WIDTH), kv.reshape(n_seq, rows, 2 * KV_WIDTH), cache_k, cache_v, state,
      u.reshape(n_seq, rows, POOL_WIDTH), bias, wpool, pscale)


def _outproj_kernel(mix_ref, x_ref, g_ref, w_ref, o_ref):
    y = jnp.dot(mix_ref[...].astype(BF16), w_ref[...], preferred_element_type=F32)
    o_ref[...] = x_ref[...] + _rms(y, g_ref[...])


def _outproj(mix, x, g, w, tm):
    m = x.shape[0]
    return pl.pallas_call(
        _outproj_kernel,
        grid=(m // tm,),
        in_specs=[
            pl.BlockSpec((tm, MIX_WIDTH), lambda i: (i, 0)),
            pl.BlockSpec((tm, D_MODEL), lambda i: (i, 0)),
            pl.BlockSpec((1, D_MODEL), lambda i: (0, 0)),
            pl.BlockSpec((MIX_WIDTH, D_MODEL), lambda i: (0, 0)),
        ],
        out_specs=pl.BlockSpec((tm, D_MODEL), lambda i: (i, 0)),
        out_shape=jax.ShapeDtypeStruct((m, D_MODEL), F32),
        compiler_params=_params(("arbitrary",)),
        name="outproj",
    )(mix, x, g, w)


def _ffn_kernel(x_ref, gpre_ref, gpost_ref, wup_ref, wdn_ref, o_ref, h_ref):
    f = pl.program_id(1)

    @pl.when(f == 0)
    def _():
        h_ref[...] = _rms(x_ref[...], gpre_ref[...]).astype(BF16)
        o_ref[...] = jnp.zeros_like(o_ref)

    a = jnp.dot(h_ref[...], wup_ref[...], preferred_element_type=F32)
    a = jnp.square(jnp.maximum(a, 0.0)).astype(BF16)
    o_ref[...] += jnp.dot(a, wdn_ref[...], preferred_element_type=F32)

    @pl.when(f == pl.num_programs(1) - 1)
    def _():
        o_ref[...] = x_ref[...] + _rms(o_ref[...], gpost_ref[...])


def _ffn(x, gpre, gpost, wup, wdn, tm, tf):
    m = x.shape[0]
    return pl.pallas_call(
        _ffn_kernel,
        grid=(m // tm, D_FF // tf),
        in_specs=[
            pl.BlockSpec((tm, D_MODEL), lambda i, f: (i, 0)),
            pl.BlockSpec((1, D_MODEL), lambda i, f: (0, 0)),
            pl.BlockSpec((1, D_MODEL), lambda i, f: (0, 0)),
            pl.BlockSpec((D_MODEL, tf), lambda i, f: (0, f)),
            pl.BlockSpec((tf, D_MODEL), lambda i, f: (f, 0)),
        ],
        out_specs=pl.BlockSpec((tm, D_MODEL), lambda i, f: (i, 0)),
        out_shape=jax.ShapeDtypeStruct((m, D_MODEL), F32),
        scratch_shapes=[pltpu.VMEM((tm, D_MODEL), BF16)],
        compiler_params=_params(("arbitrary", "arbitrary")),
        name="ffn",
    )(x, gpre, gpost, wup, wdn)


def _bias_tables():
    heads = np.arange(1, N_Q_HEADS + 1, dtype=np.float32)
    slopes = np.exp2(np.float32(-8.0) * heads / np.float32(N_Q_HEADS)).astype(np.float32)

    def table(rows, mask_prev):
        i = np.arange(rows)[:, None]
        j = np.arange(2 * WINDOW)[None, :]
        dist = i + WINDOW - j
        valid = (dist >= 0) & (dist < WINDOW)
        if mask_prev:
            valid = valid & (j >= WINDOW)
        out = np.empty((2 * N_KV_HEADS, 2 * rows, 2 * WINDOW), np.float32)
        for kh in range(N_KV_HEADS):
            for par in range(2):
                for half, head in enumerate((4 * kh + par, 4 * kh + 2 + par)):
                    bias = (-slopes[head]) * dist.astype(np.float32)
                    out[2 * kh + par, half * rows:(half + 1) * rows] = np.where(valid, bias, np.float32(NEG_BIG))
        return out

    prompt = np.stack([table(WINDOW, False), table(WINDOW, True)])
    sample = table(SAMPLE_ROWS, False)
    return jnp.asarray(prompt), jnp.asarray(sample)


def kernel(x_prompt, x_sample, cache_k, cache_v, state_pool, w_in, w_out, w_pool, pool_scale, attn_sinks,
           g_pre_mix, g_post_mix, g_pre_ffn, g_post_ffn, w_up, w_down):
    n_seq, seq, _ = x_prompt.shape
    dec_batch, dec_seq, _ = x_sample.shape
    assert seq % WINDOW == 0 and dec_seq <= SAMPLE_ROWS
    n_blk = seq // WINDOW
    bias_prompt, bias_sample = _bias_tables()

    w_in_b, w_out_b, w_pool_b = w_in.astype(BF16), w_out.astype(BF16), w_pool.astype(BF16)
    w_up_b, w_down_b = w_up.astype(BF16), w_down.astype(BF16)

    xp = x_prompt.reshape(n_seq * seq, D_MODEL)
    xs = jnp.pad(x_sample, ((0, 0), (0, SAMPLE_ROWS - dec_seq), (0, 0))).reshape(dec_batch * SAMPLE_ROWS, D_MODEL)
    ck_all = cache_k.reshape(DEPTH, dec_batch, WINDOW, KV_WIDTH)
    cv_all = cache_v.reshape(DEPTH, dec_batch, WINDOW, KV_WIDTH)
    st_all = jnp.pad(state_pool, ((0, 0), (0, 0), (1, 0), (0, 0)))

    kp_l, vp_l, pp_l, ks_l, vs_l, ps_l = [], [], [], [], [], []
    for l in range(DEPTH):
        gpm, gqm = g_pre_mix[l][None], g_post_mix[l][None]
        gpf, gqf = g_pre_ffn[l][None], g_post_ffn[l][None]
        pscale = pool_scale[l][None]

        q, kv, u = _inproj(xp, gpm, w_in_b[l], 512)
        mix = _mixer_prompt(q, kv, u, attn_sinks[l], bias_prompt, w_pool_b[l], pscale, n_seq, n_blk)
        xp = _outproj(mix, xp, gqm, w_out_b[l], 512)
        xp = _ffn(xp, gpf, gqf, w_up_b[l], w_down_b[l], 1024, 512)
        kv3 = kv.reshape(n_seq, seq, 2 * KV_WIDTH)[:, seq - WINDOW:]
        kp_l.append(kv3[..., :KV_WIDTH].reshape(n_seq, WINDOW, N_KV_HEADS, HEAD_DIM))
        vp_l.append(kv3[..., KV_WIDTH:].reshape(n_seq, WINDOW, N_KV_HEADS, HEAD_DIM))
        pp_l.append(u.reshape(n_seq, seq, POOL_WIDTH)[:, seq - POOL_STATE:])

        q, kv, u = _inproj(xs, gpm, w_in_b[l], xs.shape[0])
        mix, kn, vn, pn = _mixer_sample(q, kv, u, ck_all[l], cv_all[l], st_all[l], attn_sinks[l], bias_sample,
                                        w_pool_b[l], pscale, dec_seq)
        xs = _outproj(mix.reshape(xs.shape[0], MIX_WIDTH), xs, gqm, w_out_b[l], xs.shape[0])
        xs = _ffn(xs, gpf, gqf, w_up_b[l], w_down_b[l], xs.shape[0], 512)
        ks_l.append(kn.reshape(dec_batch, WINDOW, N_KV_HEADS, HEAD_DIM))
        vs_l.append(vn.reshape(dec_batch, WINDOW, N_KV_HEADS, HEAD_DIM))
        ps_l.append(pn)

    y_prompt = xp.reshape(n_seq, seq, D_MODEL)
    y_sample = xs.reshape(dec_batch, SAMPLE_ROWS, D_MODEL)[:, :dec_seq]
    return (y_prompt, y_sample, jnp.stack(kp_l), jnp.stack(vp_l), jnp.stack(pp_l),
            jnp.stack(ks_l), jnp.stack(vs_l), jnp.stack(ps_l))
```

```python
import functools

import jax
import jax.numpy as jnp
import numpy as np
from jax import lax
from jax.experimental import pallas as pl
from jax.experimental.pallas import tpu as pltpu

D_MODEL = 2048
DEPTH = 4
PAST_LEN = 16384
HEAD_DIM = 64
N_Q_HEADS = 16
N_KV_HEADS = 4
ATTN_WIDTH = N_Q_HEADS * HEAD_DIM
KV_WIDTH = N_KV_HEADS * HEAD_DIM
WINDOW = 128
POOL_WINDOWS = (2, 4, 8, 16)
POOL_GROUP_WIDTH = 256
POOL_WIDTH = len(POOL_WINDOWS) * POOL_GROUP_WIDTH
POOL_STATE = max(POOL_WINDOWS) - 1
POOL_PREV_ROWS = POOL_STATE + 1
MIX_WIDTH = ATTN_WIDTH + POOL_WIDTH
IN_WIDTH = ATTN_WIDTH + 2 * KV_WIDTH + POOL_WIDTH
D_FF = 4 * D_MODEL
EPS = 1e-6
Q_SCALE = HEAD_DIM ** -0.5

LANES = 128
SUBLANES = 8
NEG_BIG = -1e30
VMEM_LIMIT_BYTES = 56 * 1024 * 1024
SAMPLE_ROWS = SUBLANES

PROMPT_TM = 512
FFN_TM = 1024
WEIGHT_BLOCK = 512

BF16 = jnp.bfloat16
F32 = jnp.float32


def _rms(x, g):
    ms = jnp.mean(x * x, axis=-1, keepdims=True)
    return (x * lax.rsqrt(ms + EPS)) * g


def _params(*semantics):
    return pltpu.CompilerParams(dimension_semantics=semantics, vmem_limit_bytes=VMEM_LIMIT_BYTES)


def _layer_vec(l, width):
    return pl.BlockSpec((None, 1, width), lambda *_: (l, 0, 0))


def _last_rows(x, n):
    tail = x[x.shape[0] - 2 * SUBLANES:]
    return pltpu.roll(tail, n, axis=0)[:n]


def _inproj_prompt_kernel(tiles_per_seq, x_ref, g_ref, w_ref, kp_any, vp_any, pp_any,
                          q_ref, kv_ref, u_ref, kp_ref, vp_ref, pp_ref):
    del kp_any, vp_any, pp_any
    h = _rms(x_ref[...], g_ref[...]).astype(BF16)
    q = jnp.dot(h, w_ref[:, :ATTN_WIDTH], preferred_element_type=F32)
    q_ref[...] = q * Q_SCALE
    kv = jnp.dot(h, w_ref[:, ATTN_WIDTH:ATTN_WIDTH + 2 * KV_WIDTH], preferred_element_type=F32)
    kv_ref[...] = kv
    u = jnp.dot(h, w_ref[:, ATTN_WIDTH + 2 * KV_WIDTH:], preferred_element_type=F32)
    u_ref[...] = u

    @pl.when(pl.program_id(0) % tiles_per_seq == tiles_per_seq - 1)
    def _():
        tail = kv[kv.shape[0] - WINDOW:]
        kp_ref[...] = tail[:, :KV_WIDTH]
        vp_ref[...] = tail[:, KV_WIDTH:]
        pp_ref[...] = _last_rows(u, POOL_STATE)


def _inproj_prompt(l, x, g, w, kp, vp, pp, seq):
    m = x.shape[0]
    tm = PROMPT_TM
    tiles_per_seq = seq // tm
    state = lambda rows, width: pl.BlockSpec((None, None, rows, width), lambda i: (l, i // tiles_per_seq, 0, 0))
    return pl.pallas_call(
        functools.partial(_inproj_prompt_kernel, tiles_per_seq),
        grid=(m // tm,),
        in_specs=[
            pl.BlockSpec((tm, D_MODEL), lambda i: (i, 0)),
            _layer_vec(l, D_MODEL),
            pl.BlockSpec((D_MODEL, IN_WIDTH), lambda i: (0, 0)),
            pl.BlockSpec(memory_space=pl.ANY),
            pl.BlockSpec(memory_space=pl.ANY),
            pl.BlockSpec(memory_space=pl.ANY),
        ],
        out_specs=[
            pl.BlockSpec((tm, ATTN_WIDTH), lambda i: (i, 0)),
            pl.BlockSpec((tm, 2 * KV_WIDTH), lambda i: (i, 0)),
            pl.BlockSpec((tm, POOL_WIDTH), lambda i: (i, 0)),
            state(WINDOW, KV_WIDTH),
            state(WINDOW, KV_WIDTH),
            state(POOL_STATE, POOL_WIDTH),
        ],
        out_shape=[
            jax.ShapeDtypeStruct((m, ATTN_WIDTH), F32),
            jax.ShapeDtypeStruct((m, 2 * KV_WIDTH), F32),
            jax.ShapeDtypeStruct((m, POOL_WIDTH), F32),
            jax.ShapeDtypeStruct(kp.shape, F32),
            jax.ShapeDtypeStruct(vp.shape, F32),
            jax.ShapeDtypeStruct(pp.shape, F32),
        ],
        input_output_aliases={3: 3, 4: 4, 5: 5},
        compiler_params=_params("arbitrary"),
        name="inproj_prompt",
    )(x, g, w, kp, vp, pp)


def _inproj_sample_kernel(x_ref, g_ref, w_ref, proj_ref, wb_ref, h_ref):
    j = pl.program_id(0)

    @pl.when(j == 0)
    def _():
        h_ref[...] = _rms(x_ref[...], g_ref[...]).astype(BF16)

    wb = w_ref[...].astype(BF16)
    wb_ref[...] = wb
    p = jnp.dot(h_ref[...], wb, preferred_element_type=F32)
    proj_ref[...] = p * jnp.where(j < ATTN_WIDTH // WEIGHT_BLOCK, Q_SCALE, 1.0)


def _inproj_sample(l, x, g, w):
    m = x.shape[0]
    tn = WEIGHT_BLOCK
    return pl.pallas_call(
        _inproj_sample_kernel,
        grid=(IN_WIDTH // tn,),
        in_specs=[
            pl.BlockSpec((m, D_MODEL), lambda j: (0, 0)),
            _layer_vec(l, D_MODEL),
            pl.BlockSpec((None, D_MODEL, tn), lambda j: (l, 0, j)),
        ],
        out_specs=[
            pl.BlockSpec((m, tn), lambda j: (0, j)),
            pl.BlockSpec((D_MODEL, tn), lambda j: (0, j)),
        ],
        out_shape=[
            jax.ShapeDtypeStruct((m, IN_WIDTH), F32),
            jax.ShapeDtypeStruct((D_MODEL, IN_WIDTH), BF16),
        ],
        scratch_shapes=[pltpu.VMEM((m, D_MODEL), BF16)],
        compiler_params=_params("arbitrary"),
        name="inproj_sample",
    )(x, g, w)


def _half_padded(x2, head_in_high_half):
    lane = lax.broadcasted_iota(jnp.int32, x2.shape, 1)
    swapped = pltpu.roll(x2, HEAD_DIM, axis=1)
    in_lo, in_hi = (swapped, x2) if head_in_high_half else (x2, swapped)
    lo = jnp.where(lane < HEAD_DIM, in_lo, 0.0).astype(BF16)
    hi = jnp.where(lane >= HEAD_DIM, in_hi, 0.0).astype(BF16)
    return lo, hi


def _attention(rows, q, kk, vv, bias_at, sink_at, out_ref):
    row = lax.broadcasted_iota(jnp.int32, (2 * rows, 1), 0)
    lane = lax.broadcasted_iota(jnp.int32, (2 * rows, LANES), 1)
    for kh in range(N_KV_HEADS):
        col = (kh // 2) * LANES
        k_lo, k_hi = _half_padded(kk[:, col:col + LANES], kh % 2 == 1)
        v_lo, v_hi = _half_padded(vv[:, col:col + LANES], kh % 2 == 1)
        c0 = 2 * kh * LANES
        qq = jnp.concatenate([q[:, c0:c0 + LANES], q[:, c0 + LANES:c0 + 2 * LANES]], axis=0).astype(BF16)
        probs, recips = [], []
        for par, k_pad in enumerate((k_lo, k_hi)):
            s = lax.dot_general(qq, k_pad, (((1,), (1,)), ((), ())), preferred_element_type=F32)
            s = s + bias_at(2 * kh + par)
            sink = jnp.where(row < rows, sink_at(4 * kh + par), sink_at(4 * kh + 2 + par))
            m = jnp.maximum(jnp.max(s, axis=-1, keepdims=True), sink)
            p = jnp.exp(s - m)
            denom = jnp.sum(p, axis=-1, keepdims=True) + jnp.exp(sink - m)
            probs.append(p.astype(BF16))
            recips.append(1.0 / denom)
        o = (jnp.dot(probs[0], v_lo, preferred_element_type=F32)
             + jnp.dot(probs[1], v_hi, preferred_element_type=F32))
        o = o * jnp.where(lane < HEAD_DIM, recips[0], recips[1])
        out_ref[:, c0:c0 + LANES] = o[:rows].astype(out_ref.dtype)
        out_ref[:, c0 + LANES:c0 + 2 * LANES] = o[rows:].astype(out_ref.dtype)


def _pool(rows, u_prev, u_cur, pos, wpool_ref, pscale_ref, out_ref):
    ext = jnp.concatenate([u_prev, u_cur], axis=0)
    for gi, w in enumerate(POOL_WINDOWS):
        lo = gi * POOL_GROUP_WIDTH
        e = ext[:, lo:lo + POOL_GROUP_WIDTH]
        s, d = e, 1
        while d < w:
            s = s + pltpu.roll(s, d, axis=0)
            d *= 2
        cnt = jnp.minimum(pos + 1, w).astype(F32)
        z = s[POOL_PREV_ROWS:] / cnt - e[POOL_PREV_ROWS:]
        zz = jnp.dot(z.astype(BF16), wpool_ref[gi], preferred_element_type=F32)
        zz = zz * pscale_ref[:, lo:lo + POOL_GROUP_WIDTH]
        out_ref[:, ATTN_WIDTH + lo:ATTN_WIDTH + lo + POOL_GROUP_WIDTH] = zz.astype(out_ref.dtype)


def _mixer_prompt_kernel(l, sinks_ref, q_ref, kvp_ref, kvc_ref, up_ref, uc_ref, bias_ref, wpool_ref, pscale_ref,
                         out_ref):
    b = pl.program_id(1)
    first = (b == 0).astype(jnp.int32)
    kvp, kvc = kvp_ref[...], kvc_ref[...]
    kk = jnp.concatenate([kvp[:, :KV_WIDTH], kvc[:, :KV_WIDTH]], axis=0)
    vv = jnp.concatenate([kvp[:, KV_WIDTH:], kvc[:, KV_WIDTH:]], axis=0)
    _attention(WINDOW, q_ref[...], kk, vv, lambda i: bias_ref[first, i], lambda h: sinks_ref[l, h], out_ref)
    u_prev = jnp.where(b == 0, 0.0, up_ref[...])
    pos = b * WINDOW + lax.broadcasted_iota(jnp.int32, (WINDOW, 1), 0)
    _pool(WINDOW, u_prev, uc_ref[...], pos, wpool_ref, pscale_ref, out_ref)


def _mixer_prompt(l, q, kv, u, sinks, bias, wpool, pscale, n_seq, n_blk):
    m = q.shape[0]
    sub = WINDOW // POOL_PREV_ROWS

    def cur(n, b):
        return (n * n_blk + b, 0)

    def prev(n, b):
        return (n * n_blk + jnp.maximum(b - 1, 0), 0)

    def prev_u(n, b):
        return (jnp.maximum((n * n_blk + b) * sub - 1, 0), 0)

    return pl.pallas_call(
        functools.partial(_mixer_prompt_kernel, l),
        grid=(n_seq, n_blk),
        in_specs=[
            pl.BlockSpec(memory_space=pltpu.SMEM),
            pl.BlockSpec((WINDOW, ATTN_WIDTH), cur),
            pl.BlockSpec((WINDOW, 2 * KV_WIDTH), prev),
            pl.BlockSpec((WINDOW, 2 * KV_WIDTH), cur),
            pl.BlockSpec((POOL_PREV_ROWS, POOL_WIDTH), prev_u),
            pl.BlockSpec((WINDOW, POOL_WIDTH), cur),
            pl.BlockSpec(bias.shape, lambda n, b: (0, 0, 0, 0)),
            pl.BlockSpec((None,) + wpool.shape[1:], lambda n, b: (l, 0, 0, 0)),
            _layer_vec(l, POOL_WIDTH),
        ],
        out_specs=pl.BlockSpec((WINDOW, MIX_WIDTH), cur),
        out_shape=jax.ShapeDtypeStruct((m, MIX_WIDTH), BF16),
        compiler_params=_params("arbitrary", "arbitrary"),
        name="mixer_prompt",
    )(sinks, q, kv, kv, u, u, bias, wpool, pscale)


def _shift_in(old, new, n_new):
    r = old.shape[0]
    rolled = pltpu.roll(old, r - n_new, axis=0)
    tail = pltpu.roll(new, SUBLANES - n_new, axis=0)
    row = lax.broadcasted_iota(jnp.int32, (SUBLANES, old.shape[1]), 0)
    last = jnp.where(row < SUBLANES - n_new, rolled[r - SUBLANES:], tail)
    return jnp.concatenate([rolled[:r - SUBLANES], last], axis=0)


def _mixer_sample_kernel(l, n_new, sinks_ref, proj_ref, ck_ref, cv_ref, st_ref, bias_ref, wpool_ref, pscale_ref,
                         ks_any, vs_any, ps_any, out_ref, ko_ref, vo_ref, po_ref):
    del ks_any, vs_any, ps_any
    rows = SAMPLE_ROWS
    q = proj_ref[:, :ATTN_WIDTH]
    k_new = proj_ref[:, ATTN_WIDTH:ATTN_WIDTH + KV_WIDTH]
    v_new = proj_ref[:, ATTN_WIDTH + KV_WIDTH:ATTN_WIDTH + 2 * KV_WIDTH]
    u_cur = proj_ref[:, ATTN_WIDTH + 2 * KV_WIDTH:]
    ck, cv = ck_ref[...], cv_ref[...]
    pad = jnp.zeros((WINDOW - rows, KV_WIDTH), F32)
    kk = jnp.concatenate([ck, k_new, pad], axis=0)
    vv = jnp.concatenate([cv, v_new, pad], axis=0)
    _attention(rows, q, kk, vv, lambda i: bias_ref[i], lambda h: sinks_ref[l, h], out_ref)
    u_prev = st_ref[...]
    pos = PAST_LEN + lax.broadcasted_iota(jnp.int32, (rows, 1), 0)
    _pool(rows, u_prev, u_cur, pos, wpool_ref, pscale_ref, out_ref)
    ko_ref[...] = _shift_in(ck, k_new, n_new)
    vo_ref[...] = _shift_in(cv, v_new, n_new)
    po_ref[...] = _last_rows(_shift_in(u_prev, u_cur, n_new), POOL_STATE)


def _mixer_sample(l, proj, cache_k, cache_v, state, sinks, bias, wpool, pscale, ks, vs, ps, n_new):
    n_seq = cache_k.shape[1]
    rows = SAMPLE_ROWS
    per_seq = lambda r, width: pl.BlockSpec((None, None, r, width), lambda n: (l, n, 0, 0))
    return pl.pallas_call(
        functools.partial(_mixer_sample_kernel, l, n_new),
        grid=(n_seq,),
        in_specs=[
            pl.BlockSpec(memory_space=pltpu.SMEM),
            pl.BlockSpec((None, rows, IN_WIDTH), lambda n: (n, 0, 0)),
            per_seq(WINDOW, KV_WIDTH),
            per_seq(WINDOW, KV_WIDTH),
            per_seq(POOL_PREV_ROWS, POOL_WIDTH),
            pl.BlockSpec(bias.shape, lambda n: (0, 0, 0)),
            pl.BlockSpec((None,) + wpool.shape[1:], lambda n: (l, 0, 0, 0)),
            _layer_vec(l, POOL_WIDTH),
            pl.BlockSpec(memory_space=pl.ANY),
            pl.BlockSpec(memory_space=pl.ANY),
            pl.BlockSpec(memory_space=pl.ANY),
        ],
        out_specs=[
            pl.BlockSpec((None, rows, MIX_WIDTH), lambda n: (n, 0, 0)),
            per_seq(WINDOW, KV_WIDTH),
            per_seq(WINDOW, KV_WIDTH),
            per_seq(POOL_STATE, POOL_WIDTH),
        ],
        out_shape=[
            jax.ShapeDtypeStruct((n_seq, rows, MIX_WIDTH), F32),
            jax.ShapeDtypeStruct(ks.shape, F32),
            jax.ShapeDtypeStruct(vs.shape, F32),
            jax.ShapeDtypeStruct(ps.shape, F32),
        ],
        input_output_aliases={8: 1, 9: 2, 10: 3},
        compiler_params=_params("arbitrary"),
        name="mixer_sample",
    )(sinks, proj.reshape(n_seq, rows, IN_WIDTH), cache_k, cache_v, state, bias, wpool, pscale, ks, vs, ps)


def _outproj_prompt_kernel(mix_ref, x_ref, g_ref, w_ref, o_ref):
    y = jnp.dot(mix_ref[...], w_ref[...], preferred_element_type=F32)
    o_ref[...] = x_ref[...] + _rms(y, g_ref[...])


def _outproj_prompt(l, mix, x, g, w):
    m = x.shape[0]
    tm = PROMPT_TM
    return pl.pallas_call(
        _outproj_prompt_kernel,
        grid=(m // tm,),
        in_specs=[
            pl.BlockSpec((tm, MIX_WIDTH), lambda i: (i, 0)),
            pl.BlockSpec((tm, D_MODEL), lambda i: (i, 0)),
            _layer_vec(l, D_MODEL),
            pl.BlockSpec((MIX_WIDTH, D_MODEL), lambda i: (0, 0)),
        ],
        out_specs=pl.BlockSpec((tm, D_MODEL), lambda i: (i, 0)),
        out_shape=jax.ShapeDtypeStruct((m, D_MODEL), F32),
        compiler_params=_params("arbitrary"),
        name="outproj_prompt",
    )(mix, x, g, w)


def _outproj_sample_kernel(mix_ref, x_ref, g_ref, w_ref, o_ref, wb_ref):
    k = pl.program_id(0)

    @pl.when(k == 0)
    def _():
        o_ref[...] = jnp.zeros_like(o_ref)

    wb = w_ref[...].astype(BF16)
    wb_ref[...] = wb
    o_ref[...] += jnp.dot(mix_ref[...].astype(BF16), wb, preferred_element_type=F32)

    @pl.when(k == pl.num_programs(0) - 1)
    def _():
        o_ref[...] = x_ref[...] + _rms(o_ref[...], g_ref[...])


def _outproj_sample(l, mix, x, g, w):
    m = x.shape[0]
    tk = WEIGHT_BLOCK
    return pl.pallas_call(
        _outproj_sample_kernel,
        grid=(MIX_WIDTH // tk,),
        in_specs=[
            pl.BlockSpec((m, tk), lambda k: (0, k)),
            pl.BlockSpec((m, D_MODEL), lambda k: (0, 0)),
            _layer_vec(l, D_MODEL),
            pl.BlockSpec((None, tk, D_MODEL), lambda k: (l, k, 0)),
        ],
        out_specs=[
            pl.BlockSpec((m, D_MODEL), lambda k: (0, 0)),
            pl.BlockSpec((tk, D_MODEL), lambda k: (k, 0)),
        ],
        out_shape=[
            jax.ShapeDtypeStruct((m, D_MODEL), F32),
            jax.ShapeDtypeStruct((MIX_WIDTH, D_MODEL), BF16),
        ],
        compiler_params=_params("arbitrary"),
        name="outproj_sample",
    )(mix, x, g, w)


def _ffn_kernel(cast_weights, x_ref, gpre_ref, gpost_ref, wup_ref, wdn_ref, o_ref, *rest):
    if cast_weights:
        wupb_ref, wdnb_ref, h_ref = rest
        wup = wup_ref[...].astype(BF16)
        wdn = wdn_ref[...].astype(BF16)
        wupb_ref[...] = wup
        wdnb_ref[...] = wdn
    else:
        (h_ref,) = rest
        wup, wdn = wup_ref[...], wdn_ref[...]
    f = pl.program_id(1)

    @pl.when(f == 0)
    def _():
        h_ref[...] = _rms(x_ref[...], gpre_ref[...]).astype(BF16)
        o_ref[...] = jnp.zeros_like(o_ref)

    a = jnp.dot(h_ref[...], wup, preferred_element_type=F32)
    a = jnp.square(jnp.maximum(a, 0.0)).astype(BF16)
    o_ref[...] += jnp.dot(a, wdn, preferred_element_type=F32)

    @pl.when(f == pl.num_programs(1) - 1)
    def _():
        o_ref[...] = x_ref[...] + _rms(o_ref[...], gpost_ref[...])


def _ffn(l, x, gpre, gpost, wup, wdn, tm, cast_weights):
    m = x.shape[0]
    tf = WEIGHT_BLOCK
    if cast_weights:
        assert m == tm
        w_specs = [pl.BlockSpec((None, D_MODEL, tf), lambda i, f: (l, 0, f)),
                   pl.BlockSpec((None, tf, D_MODEL), lambda i, f: (l, f, 0))]
        extra_specs = [pl.BlockSpec((D_MODEL, tf), lambda i, f: (0, f)),
                       pl.BlockSpec((tf, D_MODEL), lambda i, f: (f, 0))]
        extra_shapes = [jax.ShapeDtypeStruct((D_MODEL, D_FF), BF16), jax.ShapeDtypeStruct((D_FF, D_MODEL), BF16)]
    else:
        w_specs = [pl.BlockSpec((D_MODEL, tf), lambda i, f: (0, f)),
                   pl.BlockSpec((tf, D_MODEL), lambda i, f: (f, 0))]
        extra_specs, extra_shapes = [], []
    return pl.pallas_call(
        functools.partial(_ffn_kernel, cast_weights),
        grid=(m // tm, D_FF // tf),
        in_specs=[
            pl.BlockSpec((tm, D_MODEL), lambda i, f: (i, 0)),
            _layer_vec(l, D_MODEL),
            _layer_vec(l, D_MODEL),
        ] + w_specs,
        out_specs=[pl.BlockSpec((tm, D_MODEL), lambda i, f: (i, 0))] + extra_specs,
        out_shape=[jax.ShapeDtypeStruct((m, D_MODEL), F32)] + extra_shapes,
        scratch_shapes=[pltpu.VMEM((tm, D_MODEL), BF16)],
        compiler_params=_params("arbitrary", "arbitrary"),
        name="ffn_sample" if cast_weights else "ffn_prompt",
    )(x, gpre, gpost, wup, wdn)


def _bias_tables():
    heads = np.arange(1, N_Q_HEADS + 1, dtype=np.float32)
    slopes = np.exp2(np.float32(-8.0) * heads / np.float32(N_Q_HEADS)).astype(np.float32)

    def table(rows, mask_prev):
        i = np.arange(rows)[:, None]
        j = np.arange(2 * WINDOW)[None, :]
        dist = i + WINDOW - j
        valid = (dist >= 0) & (dist < WINDOW)
        if mask_prev:
            valid = valid & (j >= WINDOW)
        out = np.empty((2 * N_KV_HEADS, 2 * rows, 2 * WINDOW), np.float32)
        for kh in range(N_KV_HEADS):
            for par in range(2):
                for half, head in enumerate((4 * kh + par, 4 * kh + 2 + par)):
                    bias = (-slopes[head]) * dist.astype(np.float32)
                    out[2 * kh + par, half * rows:(half + 1) * rows] = np.where(valid, bias, np.float32(NEG_BIG))
        return out

    prompt = np.stack([table(WINDOW, False), table(WINDOW, True)])
    sample = table(SAMPLE_ROWS, False)
    return jnp.asarray(prompt), jnp.asarray(sample)


def kernel(x_prompt, x_sample, cache_k, cache_v, state_pool, w_in, w_out, w_pool, pool_scale, attn_sinks,
           g_pre_mix, g_post_mix, g_pre_ffn, g_post_ffn, w_up, w_down):
    n_seq, seq, _ = x_prompt.shape
    dec_batch, dec_seq, _ = x_sample.shape
    assert seq % PROMPT_TM == 0 and (n_seq * seq) % FFN_TM == 0 and dec_seq <= SAMPLE_ROWS
    n_blk = seq // WINDOW
    bias_prompt, bias_sample = _bias_tables()

    vec = lambda p: p.reshape(DEPTH, 1, p.shape[-1])
    gpm, gqm, gpf, gqf, pscale = vec(g_pre_mix), vec(g_post_mix), vec(g_pre_ffn), vec(g_post_ffn), vec(pool_scale)
    w_pool_b = w_pool.astype(BF16)

    xp = x_prompt.reshape(n_seq * seq, D_MODEL)
    xs = jnp.pad(x_sample, ((0, 0), (0, SAMPLE_ROWS - dec_seq), (0, 0))).reshape(dec_batch * SAMPLE_ROWS, D_MODEL)
    ck_all = cache_k.reshape(DEPTH, dec_batch, WINDOW, KV_WIDTH)
    cv_all = cache_v.reshape(DEPTH, dec_batch, WINDOW, KV_WIDTH)
    st_all = jnp.pad(state_pool, ((0, 0), (0, 0), (1, 0), (0, 0)))

    kp = jnp.zeros((DEPTH, n_seq, WINDOW, KV_WIDTH), F32)
    vp = jnp.zeros((DEPTH, n_seq, WINDOW, KV_WIDTH), F32)
    pp = jnp.zeros((DEPTH, n_seq, POOL_STATE, POOL_WIDTH), F32)
    ks = jnp.zeros((DEPTH, dec_batch, WINDOW, KV_WIDTH), F32)
    vs = jnp.zeros((DEPTH, dec_batch, WINDOW, KV_WIDTH), F32)
    ps = jnp.zeros((DEPTH, dec_batch, POOL_STATE, POOL_WIDTH), F32)

    for l in range(DEPTH):
        proj, w_in_b = _inproj_sample(l, xs, gpm, w_in)
        mix, ks, vs, ps = _mixer_sample(l, proj, ck_all, cv_all, st_all, attn_sinks, bias_sample, w_pool_b, pscale,
                                        ks, vs, ps, dec_seq)
        xs, w_out_b = _outproj_sample(l, mix.reshape(xs.shape[0], MIX_WIDTH), xs, gqm, w_out)
        xs, w_up_b, w_down_b = _ffn(l, xs, gpf, gqf, w_up, w_down, xs.shape[0], True)

        q, kv, u, kp, vp, pp = _inproj_prompt(l, xp, gpm, w_in_b, kp, vp, pp, seq)
        mix = _mixer_prompt(l, q, kv, u, attn_sinks, bias_prompt, w_pool_b, pscale, n_seq, n_blk)
        xp = _outproj_prompt(l, mix, xp, gqm, w_out_b)
        (xp,) = _ffn(l, xp, gpf, gqf, w_up_b, w_down_b, FFN_TM, False)

    y_prompt = xp.reshape(n_seq, seq, D_MODEL)
    y_sample = xs.reshape(dec_batch, SAMPLE_ROWS, D_MODEL)[:, :dec_seq]
    kv_shape = (WINDOW, N_KV_HEADS, HEAD_DIM)
    return (y_prompt, y_sample, kp.reshape((DEPTH, n_seq) + kv_shape), vp.reshape((DEPTH, n_seq) + kv_shape), pp,
            ks.reshape((DEPTH, dec_batch) + kv_shape), vs.reshape((DEPTH, dec_batch) + kv_shape), ps)
```

```python
import functools

import jax
import jax.numpy as jnp
import numpy as np
from jax import lax
from jax.experimental import pallas as pl
from jax.experimental.pallas import tpu as pltpu

D_MODEL = 2048
DEPTH = 4
PAST_LEN = 16384
HEAD_DIM = 64
N_Q_HEADS = 16
N_KV_HEADS = 4
ATTN_WIDTH = N_Q_HEADS * HEAD_DIM
KV_WIDTH = N_KV_HEADS * HEAD_DIM
WINDOW = 128
POOL_WINDOWS = (2, 4, 8, 16)
POOL_GROUP_WIDTH = 256
POOL_WIDTH = len(POOL_WINDOWS) * POOL_GROUP_WIDTH
POOL_STATE = max(POOL_WINDOWS) - 1
POOL_PREV_ROWS = POOL_STATE + 1
MIX_WIDTH = ATTN_WIDTH + POOL_WIDTH
IN_WIDTH = ATTN_WIDTH + 2 * KV_WIDTH + POOL_WIDTH
D_FF = 4 * D_MODEL
EPS = 1e-6
Q_SCALE = HEAD_DIM ** -0.5

LANES = 128
SUBLANES = 8
NEG_BIG = -1e30
VMEM_LIMIT_BYTES = 56 * 1024 * 1024
SAMPLE_ROWS = SUBLANES

FFN_TM = 1024
WEIGHT_BLOCK = 512
MIX_TM = 256
KV_PADS = 4

BF16 = jnp.bfloat16
F32 = jnp.float32


def _rms(x, g):
    ms = jnp.mean(x * x, axis=-1, keepdims=True)
    return (x * lax.rsqrt(ms + EPS)) * g


def _params(*semantics):
    return pltpu.CompilerParams(dimension_semantics=semantics, vmem_limit_bytes=VMEM_LIMIT_BYTES)


def _layer_vec(l, width):
    return pl.BlockSpec((None, 1, width), lambda *_: (l, 0, 0))


def _resident(shape, index_map):
    return pl.BlockSpec(shape, index_map, pipeline_mode=pl.Buffered(1))


def _last_rows(x, n):
    tail = x[x.shape[0] - 2 * SUBLANES:]
    return pltpu.roll(tail, n, axis=0)[:n]


def _half_padded(x2, head_in_high_half):
    lane = lax.broadcasted_iota(jnp.int32, x2.shape, 1)
    swapped = pltpu.roll(x2, HEAD_DIM, axis=1)
    in_lo, in_hi = (swapped, x2) if head_in_high_half else (x2, swapped)
    lo = jnp.where(lane < HEAD_DIM, in_lo, 0.0).astype(BF16)
    hi = jnp.where(lane >= HEAD_DIM, in_hi, 0.0).astype(BF16)
    return lo, hi


def _kv_pads(k, v, kh):
    col = (kh // 2) * LANES
    high = kh % 2 == 1
    return _half_padded(k[:, col:col + LANES], high) + _half_padded(v[:, col:col + LANES], high)


def _attention(rows, q_at, pads_at, bias_at, sink_at, out_ref):
    row = lax.broadcasted_iota(jnp.int32, (2 * rows, 1), 0)
    lane = lax.broadcasted_iota(jnp.int32, (2 * rows, LANES), 1)
    for kh in range(N_KV_HEADS):
        k_lo, k_hi, v_lo, v_hi = pads_at(kh)
        c0 = 2 * kh * LANES
        qq = jnp.concatenate([q_at(c0), q_at(c0 + LANES)], axis=0).astype(BF16)
        probs, recips = [], []
        for par, k_pad in enumerate((k_lo, k_hi)):
            s = lax.dot_general(qq, k_pad, (((1,), (1,)), ((), ())), preferred_element_type=F32)
            s = s + bias_at(2 * kh + par)
            sink = jnp.where(row < rows, sink_at(4 * kh + par), sink_at(4 * kh + 2 + par))
            m = jnp.maximum(jnp.max(s, axis=-1, keepdims=True), sink)
            p = jnp.exp(s - m)
            denom = jnp.sum(p, axis=-1, keepdims=True) + jnp.exp(sink - m)
            probs.append(p.astype(BF16))
            recips.append(1.0 / denom)
        o = (jnp.dot(probs[0], v_lo, preferred_element_type=F32)
             + jnp.dot(probs[1], v_hi, preferred_element_type=F32))
        o = o * jnp.where(lane < HEAD_DIM, recips[0], recips[1])
        out_ref[:, c0:c0 + LANES] = o[:rows].astype(out_ref.dtype)
        out_ref[:, c0 + LANES:c0 + 2 * LANES] = o[rows:].astype(out_ref.dtype)


def _pool(rows, u_prev, u_cur, pos, wpool_ref, pscale_ref, out_ref):
    ext = jnp.concatenate([u_prev, u_cur], axis=0)
    for gi, w in enumerate(POOL_WINDOWS):
        lo = gi * POOL_GROUP_WIDTH
        e = ext[:, lo:lo + POOL_GROUP_WIDTH]
        s, d = e, 1
        while d < w:
            s = s + pltpu.roll(s, d, axis=0)
            d *= 2
        cnt = jnp.minimum(pos + 1, w).astype(F32)
        z = s[POOL_PREV_ROWS:] / cnt - e[POOL_PREV_ROWS:]
        zz = jnp.dot(z.astype(BF16), wpool_ref[gi], preferred_element_type=F32)
        zz = zz * pscale_ref[:, lo:lo + POOL_GROUP_WIDTH]
        out_ref[:, ATTN_WIDTH + lo:ATTN_WIDTH + lo + POOL_GROUP_WIDTH] = zz.astype(out_ref.dtype)


def _mixer_prompt_kernel(l, tiles_per_seq, sinks_ref, x_ref, gpre_ref, gpost_ref, win_ref, wout_ref,
                         bias_ref, wpool_ref, pscale_ref, kp_any, vp_any, pp_any,
                         out_ref, kp_ref, vp_ref, pp_ref, q_s, kv_s, u_s, mix_s, pads_s, utail_s):
    del kp_any, vp_any, pp_any
    s = pl.program_id(0)
    tile = s % tiles_per_seq
    blocks = MIX_TM // WINDOW

    @pl.when(s == 0)
    def _():
        pads_s[...] = jnp.zeros_like(pads_s)
        utail_s[...] = jnp.zeros_like(utail_s)

    h = _rms(x_ref[...], gpre_ref[...]).astype(BF16)
    q_s[...] = jnp.dot(h, win_ref[:, :ATTN_WIDTH], preferred_element_type=F32) * Q_SCALE
    kv_s[...] = jnp.dot(h, win_ref[:, ATTN_WIDTH:ATTN_WIDTH + 2 * KV_WIDTH], preferred_element_type=F32)
    u_s[...] = jnp.dot(h, win_ref[:, ATTN_WIDTH + 2 * KV_WIDTH:], preferred_element_type=F32)

    seq_start = tile == 0
    prev_pads = [pads_s[i] for i in range(KV_PADS * N_KV_HEADS)]
    u_prev = jnp.where(seq_start, 0.0, utail_s[...])
    for j in range(blocks):
        r0 = j * WINDOW
        rows = pl.ds(r0, WINDOW)
        cur_pads = []

        def pads_at(kh):
            cur = _kv_pads(kv_s[rows, :KV_WIDTH], kv_s[rows, KV_WIDTH:], kh)
            cur_pads.extend(cur)
            return [jnp.concatenate([p, c], axis=0) for p, c in zip(prev_pads[KV_PADS * kh:], cur)]

        if j == 0:
            first = seq_start.astype(jnp.int32)
            bias_at = lambda i: bias_ref[first, i]
        else:
            bias_at = lambda i: bias_ref[0, i]
        mix_ref = mix_s.at[rows]
        _attention(WINDOW, lambda c: q_s[rows, c:c + LANES], pads_at, bias_at, lambda hd: sinks_ref[l, hd], mix_ref)
        prev_pads = cur_pads
        u_cur = u_s[rows, :]
        pos = (tile * blocks + j) * WINDOW + lax.broadcasted_iota(jnp.int32, (WINDOW, 1), 0)
        _pool(WINDOW, u_prev, u_cur, pos, wpool_ref, pscale_ref, mix_ref)
        u_prev = u_cur[WINDOW - POOL_PREV_ROWS:]
    for i, p in enumerate(prev_pads):
        pads_s[i] = p
    utail_s[...] = u_prev

    y = jnp.dot(mix_s[...], wout_ref[...], preferred_element_type=F32)
    out_ref[...] = x_ref[...] + _rms(y, gpost_ref[...])

    @pl.when(tile == tiles_per_seq - 1)
    def _():
        kp_ref[...] = kv_s[MIX_TM - WINDOW:, :KV_WIDTH]
        vp_ref[...] = kv_s[MIX_TM - WINDOW:, KV_WIDTH:]
        pp_ref[...] = _last_rows(u_s[MIX_TM - 2 * SUBLANES:, :], POOL_STATE)


def _mixer_prompt(l, x, sinks, gpre, gpost, w_in_b, w_out_b, bias, wpool, pscale, kp, vp, pp, seq):
    m = x.shape[0]
    tm = MIX_TM
    tiles_per_seq = seq // tm
    state = lambda rows, width: pl.BlockSpec((None, None, rows, width), lambda s: (l, s // tiles_per_seq, 0, 0))
    return pl.pallas_call(
        functools.partial(_mixer_prompt_kernel, l, tiles_per_seq),
        grid=(m // tm,),
        in_specs=[
            pl.BlockSpec(memory_space=pltpu.SMEM),
            pl.BlockSpec((tm, D_MODEL), lambda s: (s, 0)),
            _layer_vec(l, D_MODEL),
            _layer_vec(l, D_MODEL),
            _resident((D_MODEL, IN_WIDTH), lambda s: (0, 0)),
            _resident((MIX_WIDTH, D_MODEL), lambda s: (0, 0)),
            _resident(bias.shape, lambda s: (0, 0, 0, 0)),
            _resident((None,) + wpool.shape[1:], lambda s: (l, 0, 0, 0)),
            _layer_vec(l, POOL_WIDTH),
            pl.BlockSpec(memory_space=pl.ANY),
            pl.BlockSpec(memory_space=pl.ANY),
            pl.BlockSpec(memory_space=pl.ANY),
        ],
        out_specs=[
            pl.BlockSpec((tm, D_MODEL), lambda s: (s, 0)),
            state(WINDOW, KV_WIDTH),
            state(WINDOW, KV_WIDTH),
            state(POOL_STATE, POOL_WIDTH),
        ],
        out_shape=[
            jax.ShapeDtypeStruct((m, D_MODEL), F32),
            jax.ShapeDtypeStruct(kp.shape, F32),
            jax.ShapeDtypeStruct(vp.shape, F32),
            jax.ShapeDtypeStruct(pp.shape, F32),
        ],
        scratch_shapes=[
            pltpu.VMEM((tm, ATTN_WIDTH), F32),
            pltpu.VMEM((tm, 2 * KV_WIDTH), F32),
            pltpu.VMEM((tm, POOL_WIDTH), F32),
            pltpu.VMEM((tm, MIX_WIDTH), BF16),
            pltpu.VMEM((KV_PADS * N_KV_HEADS, WINDOW, LANES), BF16),
            pltpu.VMEM((POOL_PREV_ROWS, POOL_WIDTH), F32),
        ],
        input_output_aliases={9: 1, 10: 2, 11: 3},
        compiler_params=_params("arbitrary"),
        name="mixer_prompt",
    )(sinks, x, gpre, gpost, w_in_b, w_out_b, bias, wpool, pscale, kp, vp, pp)


def _inproj_sample_kernel(x_ref, g_ref, w_ref, proj_ref, wb_ref, h_ref):
    j = pl.program_id(0)

    @pl.when(j == 0)
    def _():
        h_ref[...] = _rms(x_ref[...], g_ref[...]).astype(BF16)

    wb = w_ref[...].astype(BF16)
    wb_ref[...] = wb
    p = jnp.dot(h_ref[...], wb, preferred_element_type=F32)
    proj_ref[...] = p * jnp.where(j < ATTN_WIDTH // WEIGHT_BLOCK, Q_SCALE, 1.0)


def _inproj_sample(l, x, g, w):
    m = x.shape[0]
    tn = WEIGHT_BLOCK
    return pl.pallas_call(
        _inproj_sample_kernel,
        grid=(IN_WIDTH // tn,),
        in_specs=[
            pl.BlockSpec((m, D_MODEL), lambda j: (0, 0)),
            _layer_vec(l, D_MODEL),
            pl.BlockSpec((None, D_MODEL, tn), lambda j: (l, 0, j)),
        ],
        out_specs=[
            pl.BlockSpec((m, tn), lambda j: (0, j)),
            pl.BlockSpec((D_MODEL, tn), lambda j: (0, j)),
        ],
        out_shape=[
            jax.ShapeDtypeStruct((m, IN_WIDTH), F32),
            jax.ShapeDtypeStruct((D_MODEL, IN_WIDTH), BF16),
        ],
        scratch_shapes=[pltpu.VMEM((m, D_MODEL), BF16)],
        compiler_params=_params("arbitrary"),
        name="inproj_sample",
    )(x, g, w)


def _shift_in(old, new, n_new):
    r = old.shape[0]
    rolled = pltpu.roll(old, r - n_new, axis=0)
    tail = pltpu.roll(new, SUBLANES - n_new, axis=0)
    row = lax.broadcasted_iota(jnp.int32, (SUBLANES, old.shape[1]), 0)
    last = jnp.where(row < SUBLANES - n_new, rolled[r - SUBLANES:], tail)
    return jnp.concatenate([rolled[:r - SUBLANES], last], axis=0)


def _mixer_sample_kernel(l, n_new, sinks_ref, proj_ref, ck_ref, cv_ref, st_ref, bias_ref, wpool_ref, pscale_ref,
                         ks_any, vs_any, ps_any, out_ref, ko_ref, vo_ref, po_ref):
    del ks_any, vs_any, ps_any
    rows = SAMPLE_ROWS
    k_new = proj_ref[:, ATTN_WIDTH:ATTN_WIDTH + KV_WIDTH]
    v_new = proj_ref[:, ATTN_WIDTH + KV_WIDTH:ATTN_WIDTH + 2 * KV_WIDTH]
    u_cur = proj_ref[:, ATTN_WIDTH + 2 * KV_WIDTH:]
    ck, cv = ck_ref[...], cv_ref[...]
    pad = jnp.zeros((WINDOW - rows, KV_WIDTH), F32)
    kk = jnp.concatenate([ck, k_new, pad], axis=0)
    vv = jnp.concatenate([cv, v_new, pad], axis=0)
    _attention(rows, lambda c: proj_ref[:, c:c + LANES], lambda kh: _kv_pads(kk, vv, kh), lambda i: bias_ref[i],
               lambda hd: sinks_ref[l, hd], out_ref)
    u_prev = st_ref[...]
    pos = PAST_LEN + lax.broadcasted_iota(jnp.int32, (rows, 1), 0)
    _pool(rows, u_prev, u_cur, pos, wpool_ref, pscale_ref, out_ref)
    ko_ref[...] = _shift_in(ck, k_new, n_new)
    vo_ref[...] = _shift_in(cv, v_new, n_new)
    po_ref[...] = _last_rows(_shift_in(u_prev, u_cur, n_new), POOL_STATE)


def _mixer_sample(l, proj, cache_k, cache_v, state, sinks, bias, wpool, pscale, ks, vs, ps, n_new):
    n_seq = cache_k.shape[1]
    rows = SAMPLE_ROWS
    per_seq = lambda r, width: pl.BlockSpec((None, None, r, width), lambda n: (l, n, 0, 0))
    return pl.pallas_call(
        functools.partial(_mixer_sample_kernel, l, n_new),
        grid=(n_seq,),
        in_specs=[
            pl.BlockSpec(memory_space=pltpu.SMEM),
            pl.BlockSpec((None, rows, IN_WIDTH), lambda n: (n, 0, 0)),
            per_seq(WINDOW, KV_WIDTH),
            per_seq(WINDOW, KV_WIDTH),
            per_seq(POOL_PREV_ROWS, POOL_WIDTH),
            pl.BlockSpec(bias.shape, lambda n: (0, 0, 0)),
            pl.BlockSpec((None,) + wpool.shape[1:], lambda n: (l, 0, 0, 0)),
            _layer_vec(l, POOL_WIDTH),
            pl.BlockSpec(memory_space=pl.ANY),
            pl.BlockSpec(memory_space=pl.ANY),
            pl.BlockSpec(memory_space=pl.ANY),
        ],
        out_specs=[
            pl.BlockSpec((None, rows, MIX_WIDTH), lambda n: (n, 0, 0)),
            per_seq(WINDOW, KV_WIDTH),
            per_seq(WINDOW, KV_WIDTH),
            per_seq(POOL_STATE, POOL_WIDTH),
        ],
        out_shape=[
            jax.ShapeDtypeStruct((n_seq, rows, MIX_WIDTH), F32),
            jax.ShapeDtypeStruct(ks.shape, F32),
            jax.ShapeDtypeStruct(vs.shape, F32),
            jax.ShapeDtypeStruct(ps.shape, F32),
        ],
        input_output_aliases={8: 1, 9: 2, 10: 3},
        compiler_params=_params("arbitrary"),
        name="mixer_sample",
    )(sinks, proj.reshape(n_seq, rows, IN_WIDTH), cache_k, cache_v, state, bias, wpool, pscale, ks, vs, ps)


def _outproj_sample_kernel(mix_ref, x_ref, g_ref, w_ref, o_ref, wb_ref):
    k = pl.program_id(0)

    @pl.when(k == 0)
    def _():
        o_ref[...] = jnp.zeros_like(o_ref)

    wb = w_ref[...].astype(BF16)
    wb_ref[...] = wb
    o_ref[...] += jnp.dot(mix_ref[...].astype(BF16), wb, preferred_element_type=F32)

    @pl.when(k == pl.num_programs(0) - 1)
    def _():
        o_ref[...] = x_ref[...] + _rms(o_ref[...], g_ref[...])


def _outproj_sample(l, mix, x, g, w):
    m = x.shape[0]
    tk = WEIGHT_BLOCK
    return pl.pallas_call(
        _outproj_sample_kernel,
        grid=(MIX_WIDTH // tk,),
        in_specs=[
            pl.BlockSpec((m, tk), lambda k: (0, k)),
            pl.BlockSpec((m, D_MODEL), lambda k: (0, 0)),
            _layer_vec(l, D_MODEL),
            pl.BlockSpec((None, tk, D_MODEL), lambda k: (l, k, 0)),
        ],
        out_specs=[
            pl.BlockSpec((m, D_MODEL), lambda k: (0, 0)),
            pl.BlockSpec((tk, D_MODEL), lambda k: (k, 0)),
        ],
        out_shape=[
            jax.ShapeDtypeStruct((m, D_MODEL), F32),
            jax.ShapeDtypeStruct((MIX_WIDTH, D_MODEL), BF16),
        ],
        compiler_params=_params("arbitrary"),
        name="outproj_sample",
    )(mix, x, g, w)


def _ffn_kernel(cast_weights, x_ref, gpre_ref, gpost_ref, wup_ref, wdn_ref, o_ref, *rest):
    f = pl.program_id(1)
    h_ref = rest[-1]

    @pl.when(f == 0)
    def _():
        h_ref[...] = _rms(x_ref[...], gpre_ref[...]).astype(BF16)
        o_ref[...] = jnp.zeros_like(o_ref)

    if cast_weights:
        wupb_ref, wdnb_ref = rest[:2]
        wupb_ref[...] = wup_ref[...].astype(BF16)
        wdnb_ref[...] = wdn_ref[...].astype(BF16)
        wup_ref, wdn_ref = wupb_ref, wdnb_ref
    a = jnp.dot(h_ref[...], wup_ref[...], preferred_element_type=F32)
    a = jnp.square(jnp.maximum(a, 0.0)).astype(BF16)
    o_ref[...] += jnp.dot(a, wdn_ref[...], preferred_element_type=F32)

    @pl.when(f == pl.num_programs(1) - 1)
    def _():
        o_ref[...] = x_ref[...] + _rms(o_ref[...], gpost_ref[...])


def _ffn(l, x, gpre, gpost, wup, wdn, tm, cast_weights):
    m = x.shape[0]
    tf = WEIGHT_BLOCK
    if cast_weights:
        assert m == tm
        w_specs = [pl.BlockSpec((None, D_MODEL, tf), lambda i, f: (l, 0, f)),
                   pl.BlockSpec((None, tf, D_MODEL), lambda i, f: (l, f, 0))]
        extra_specs = [pl.BlockSpec((D_MODEL, tf), lambda i, f: (0, f)),
                       pl.BlockSpec((tf, D_MODEL), lambda i, f: (f, 0))]
        extra_shapes = [jax.ShapeDtypeStruct((D_MODEL, D_FF), BF16), jax.ShapeDtypeStruct((D_FF, D_MODEL), BF16)]
    else:
        w_specs = [pl.BlockSpec((D_MODEL, tf), lambda i, f: (0, f)),
                   pl.BlockSpec((tf, D_MODEL), lambda i, f: (f, 0))]
        extra_specs, extra_shapes = [], []
    return pl.pallas_call(
        functools.partial(_ffn_kernel, cast_weights),
        grid=(m // tm, D_FF // tf),
        in_specs=[
            pl.BlockSpec((tm, D_MODEL), lambda i, f: (i, 0)),
            _layer_vec(l, D_MODEL),
            _layer_vec(l, D_MODEL),
        ] + w_specs,
        out_specs=[pl.BlockSpec((tm, D_MODEL), lambda i, f: (i, 0))] + extra_specs,
        out_shape=[jax.ShapeDtypeStruct((m, D_MODEL), F32)] + extra_shapes,
        scratch_shapes=[pltpu.VMEM((tm, D_MODEL), BF16)],
        compiler_params=_params("arbitrary", "arbitrary"),
        name="ffn_sample" if cast_weights else "ffn_prompt",
    )(x, gpre, gpost, wup, wdn)


def _bias_tables():
    heads = np.arange(1, N_Q_HEADS + 1, dtype=np.float32)
    slopes = np.exp2(np.float32(-8.0) * heads / np.float32(N_Q_HEADS)).astype(np.float32)

    def table(rows, mask_prev):
        i = np.arange(rows)[:, None]
        j = np.arange(2 * WINDOW)[None, :]
        dist = i + WINDOW - j
        valid = (dist >= 0) & (dist < WINDOW)
        if mask_prev:
            valid = valid & (j >= WINDOW)
        out = np.empty((2 * N_KV_HEADS, 2 * rows, 2 * WINDOW), np.float32)
        for kh in range(N_KV_HEADS):
            for par in range(2):
                for half, head in enumerate((4 * kh + par, 4 * kh + 2 + par)):
                    bias = (-slopes[head]) * dist.astype(np.float32)
                    out[2 * kh + par, half * rows:(half + 1) * rows] = np.where(valid, bias, np.float32(NEG_BIG))
        return out

    prompt = np.stack([table(WINDOW, False), table(WINDOW, True)])
    sample = table(SAMPLE_ROWS, False)
    return jnp.asarray(prompt), jnp.asarray(sample)


def kernel(x_prompt, x_sample, cache_k, cache_v, state_pool, w_in, w_out, w_pool, pool_scale, attn_sinks,
           g_pre_mix, g_post_mix, g_pre_ffn, g_post_ffn, w_up, w_down):
    n_seq, seq, _ = x_prompt.shape
    dec_batch, dec_seq, _ = x_sample.shape
    assert seq % MIX_TM == 0 and (n_seq * seq) % FFN_TM == 0 and dec_seq <= SAMPLE_ROWS
    bias_prompt, bias_sample = _bias_tables()

    vec = lambda p: p.reshape(DEPTH, 1, p.shape[-1])
    gpm, gqm, gpf, gqf, pscale = vec(g_pre_mix), vec(g_post_mix), vec(g_pre_ffn), vec(g_post_ffn), vec(pool_scale)
    w_pool_b = w_pool.astype(BF16)

    xp = x_prompt.reshape(n_seq * seq, D_MODEL)
    xs = jnp.pad(x_sample, ((0, 0), (0, SAMPLE_ROWS - dec_seq), (0, 0))).reshape(dec_batch * SAMPLE_ROWS, D_MODEL)
    ck_all = cache_k.reshape(DEPTH, dec_batch, WINDOW, KV_WIDTH)
    cv_all = cache_v.reshape(DEPTH, dec_batch, WINDOW, KV_WIDTH)
    st_all = jnp.pad(state_pool, ((0, 0), (0, 0), (1, 0), (0, 0)))

    kp = jnp.zeros((DEPTH, n_seq, WINDOW, KV_WIDTH), F32)
    vp = jnp.zeros((DEPTH, n_seq, WINDOW, KV_WIDTH), F32)
    pp = jnp.zeros((DEPTH, n_seq, POOL_STATE, POOL_WIDTH), F32)
    ks = jnp.zeros((DEPTH, dec_batch, WINDOW, KV_WIDTH), F32)
    vs = jnp.zeros((DEPTH, dec_batch, WINDOW, KV_WIDTH), F32)
    ps = jnp.zeros((DEPTH, dec_batch, POOL_STATE, POOL_WIDTH), F32)

    for l in range(DEPTH):
        proj, w_in_b = _inproj_sample(l, xs, gpm, w_in)
        mix, ks, vs, ps = _mixer_sample(l, proj, ck_all, cv_all, st_all, attn_sinks, bias_sample, w_pool_b, pscale,
                                        ks, vs, ps, dec_seq)
        xs, w_out_b = _outproj_sample(l, mix.reshape(xs.shape[0], MIX_WIDTH), xs, gqm, w_out)
        xs, w_up_b, w_down_b = _ffn(l, xs, gpf, gqf, w_up, w_down, xs.shape[0], True)

        xp, kp, vp, pp = _mixer_prompt(l, xp, attn_sinks, gpm, gqm, w_in_b, w_out_b, bias_prompt, w_pool_b, pscale,
                                       kp, vp, pp, seq)
        (xp,) = _ffn(l, xp, gpf, gqf, w_up_b, w_down_b, FFN_TM, False)

    y_prompt = xp.reshape(n_seq, seq, D_MODEL)
    y_sample = xs.reshape(dec_batch, SAMPLE_ROWS, D_MODEL)[:, :dec_seq]
    kv_shape = (WINDOW, N_KV_HEADS, HEAD_DIM)
    return (y_prompt, y_sample, kp.reshape((DEPTH, n_seq) + kv_shape), vp.reshape((DEPTH, n_seq) + kv_shape), pp,
            ks.reshape((DEPTH, dec_batch) + kv_shape), vs.reshape((DEPTH, dec_batch) + kv_shape), ps)
```

```python
import functools

import jax
import jax.numpy as jnp
import numpy as np
from jax import lax
from jax.experimental import pallas as pl
from jax.experimental.pallas import tpu as pltpu

D_MODEL = 2048
DEPTH = 4
PAST_LEN = 16384
HEAD_DIM = 64
N_Q_HEADS = 16
N_KV_HEADS = 4
ATTN_WIDTH = N_Q_HEADS * HEAD_DIM
KV_WIDTH = N_KV_HEADS * HEAD_DIM
WINDOW = 128
POOL_WINDOWS = (2, 4, 8, 16)
POOL_GROUP_WIDTH = 256
POOL_WIDTH = len(POOL_WINDOWS) * POOL_GROUP_WIDTH
POOL_STATE = max(POOL_WINDOWS) - 1
POOL_PREV_ROWS = POOL_STATE + 1
MIX_WIDTH = ATTN_WIDTH + POOL_WIDTH
IN_WIDTH = ATTN_WIDTH + 2 * KV_WIDTH + POOL_WIDTH
D_FF = 4 * D_MODEL
EPS = 1e-6
Q_SCALE = HEAD_DIM ** -0.5

LANES = 128
SUBLANES = 8
NEG_BIG = -1e30
VMEM_LIMIT_BYTES = 56 * 1024 * 1024
SAMPLE_ROWS = SUBLANES

FFN_TM = 512
FFN_TF = 1024
STAT_ROWS = 256
APPLY_ROWS = 64
SAMPLE_GROUP = 4
WEIGHT_BLOCK = 512
MIX_TM = 256
KV_PADS = 4

BF16 = jnp.bfloat16
F32 = jnp.float32


def _rms(x, g):
    ms = jnp.mean(x * x, axis=-1, keepdims=True)
    return (x * lax.rsqrt(ms + EPS)) * g


def _params(*semantics):
    return pltpu.CompilerParams(dimension_semantics=semantics, vmem_limit_bytes=VMEM_LIMIT_BYTES)


def _layer_vec(l, width):
    return pl.BlockSpec((None, 1, width), lambda *_: (l, 0, 0))


def _resident(shape, index_map):
    return pl.BlockSpec(shape, index_map, pipeline_mode=pl.Buffered(1))


def _last_rows(x, n):
    tail = x[x.shape[0] - 2 * SUBLANES:]
    return pltpu.roll(tail, n, axis=0)[:n]


def _half_padded(x2, head_in_high_half):
    lane = lax.broadcasted_iota(jnp.int32, x2.shape, 1)
    swapped = pltpu.roll(x2, HEAD_DIM, axis=1)
    in_lo, in_hi = (swapped, x2) if head_in_high_half else (x2, swapped)
    lo = jnp.where(lane < HEAD_DIM, in_lo, 0.0).astype(BF16)
    hi = jnp.where(lane >= HEAD_DIM, in_hi, 0.0).astype(BF16)
    return lo, hi


def _kv_pads(k, v, kh):
    col = (kh // 2) * LANES
    high = kh % 2 == 1
    return _half_padded(k[:, col:col + LANES], high) + _half_padded(v[:, col:col + LANES], high)


def _attention_head(kh, rows, q_at, pads, bias_at, sink_at, out_ref):
    k_lo, k_hi, v_lo, v_hi = pads
    row = lax.broadcasted_iota(jnp.int32, (2 * rows, 1), 0)
    lane = lax.broadcasted_iota(jnp.int32, (2 * rows, LANES), 1)
    c0 = 2 * kh * LANES
    qq = jnp.concatenate([q_at(c0), q_at(c0 + LANES)], axis=0).astype(BF16)
    probs, recips = [], []
    for par, k_pad in enumerate((k_lo, k_hi)):
        s = lax.dot_general(qq, k_pad, (((1,), (1,)), ((), ())), preferred_element_type=F32)
        s = s + bias_at(2 * kh + par)
        sink = jnp.where(row < rows, sink_at(4 * kh + par), sink_at(4 * kh + 2 + par))
        m = jnp.maximum(jnp.max(s, axis=-1, keepdims=True), sink)
        p = jnp.exp(s - m)
        denom = jnp.sum(p, axis=-1, keepdims=True) + jnp.exp(sink - m)
        probs.append(p.astype(BF16))
        recips.append(1.0 / denom)
    o = (jnp.dot(probs[0], v_lo, preferred_element_type=F32)
         + jnp.dot(probs[1], v_hi, preferred_element_type=F32))
    o = o * jnp.where(lane < HEAD_DIM, recips[0], recips[1])
    out_ref[:, c0:c0 + LANES] = o[:rows].astype(out_ref.dtype)
    out_ref[:, c0 + LANES:c0 + 2 * LANES] = o[rows:].astype(out_ref.dtype)


def _pool(rows, u_prev, u_cur, pos, wpool_ref, pscale_ref, out_ref):
    ext = jnp.concatenate([u_prev, u_cur], axis=0)
    for gi, w in enumerate(POOL_WINDOWS):
        lo = gi * POOL_GROUP_WIDTH
        e = ext[:, lo:lo + POOL_GROUP_WIDTH]
        s, d = e, 1
        while d < w:
            s = s + pltpu.roll(s, d, axis=0)
            d *= 2
        cnt = jnp.minimum(pos + 1, w).astype(F32)
        z = s[POOL_PREV_ROWS:] / cnt - e[POOL_PREV_ROWS:]
        zz = jnp.dot(z.astype(BF16), wpool_ref[gi], preferred_element_type=F32)
        zz = zz * pscale_ref[:, lo:lo + POOL_GROUP_WIDTH]
        out_ref[:, ATTN_WIDTH + lo:ATTN_WIDTH + lo + POOL_GROUP_WIDTH] = zz.astype(out_ref.dtype)


def _mixer_prompt_kernel(l, tiles_per_seq, sinks_ref, x_ref, gpre_ref, gpost_ref, win_ref, wout_ref,
                         bias_ref, wpool_ref, pscale_ref, kp_any, vp_any, pp_any,
                         out_ref, kp_ref, vp_ref, pp_ref, q_s, kv_s, u_s, mix_s, pads_s, utail_s):
    del kp_any, vp_any, pp_any
    s = pl.program_id(0)
    tile = s % tiles_per_seq
    blocks = MIX_TM // WINDOW

    @pl.when(s == 0)
    def _():
        pads_s[...] = jnp.zeros_like(pads_s)
        utail_s[...] = jnp.zeros_like(utail_s)

    h = _rms(x_ref[...], gpre_ref[...]).astype(BF16)
    q_s[...] = jnp.dot(h, win_ref[:, :ATTN_WIDTH], preferred_element_type=F32) * Q_SCALE
    kv_s[...] = jnp.dot(h, win_ref[:, ATTN_WIDTH:ATTN_WIDTH + 2 * KV_WIDTH], preferred_element_type=F32)
    u_s[...] = jnp.dot(h, win_ref[:, ATTN_WIDTH + 2 * KV_WIDTH:], preferred_element_type=F32)

    seq_start = tile == 0
    first = seq_start.astype(jnp.int32)
    block_rows = [pl.ds(j * WINDOW, WINDOW) for j in range(blocks)]
    prev_pads = [pads_s[i] for i in range(KV_PADS * N_KV_HEADS)]
    for kh in range(N_KV_HEADS):
        pads = prev_pads[KV_PADS * kh:KV_PADS * (kh + 1)]
        for j, rows in enumerate(block_rows):
            cur = _kv_pads(kv_s[rows, :KV_WIDTH], kv_s[rows, KV_WIDTH:], kh)
            both = [jnp.concatenate([p, c], axis=0) for p, c in zip(pads, cur)]
            bias_at = (lambda i: bias_ref[first, i]) if j == 0 else (lambda i: bias_ref[0, i])
            _attention_head(kh, WINDOW, lambda c: q_s[rows, c:c + LANES], both, bias_at, lambda hd: sinks_ref[l, hd],
                            mix_s.at[rows])
            pads = cur
        for i, p in enumerate(pads):
            pads_s[KV_PADS * kh + i] = p

    u_prev = jnp.where(seq_start, 0.0, utail_s[...])
    for j, rows in enumerate(block_rows):
        u_cur = u_s[rows, :]
        pos = (tile * blocks + j) * WINDOW + lax.broadcasted_iota(jnp.int32, (WINDOW, 1), 0)
        _pool(WINDOW, u_prev, u_cur, pos, wpool_ref, pscale_ref, mix_s.at[rows])
        u_prev = u_cur[WINDOW - POOL_PREV_ROWS:]
    utail_s[...] = u_prev

    y = jnp.dot(mix_s[...], wout_ref[...], preferred_element_type=F32)
    out_ref[...] = x_ref[...] + _rms(y, gpost_ref[...])

    @pl.when(tile == tiles_per_seq - 1)
    def _():
        kp_ref[...] = kv_s[MIX_TM - WINDOW:, :KV_WIDTH]
        vp_ref[...] = kv_s[MIX_TM - WINDOW:, KV_WIDTH:]
        pp_ref[...] = _last_rows(u_s[MIX_TM - 2 * SUBLANES:, :], POOL_STATE)


def _mixer_prompt(l, x, sinks, gpre, gpost, w_in_b, w_out_b, bias, wpool, pscale, kp, vp, pp, seq):
    m = x.shape[0]
    tm = MIX_TM
    tiles_per_seq = seq // tm
    state = lambda rows, width: pl.BlockSpec((None, None, rows, width), lambda s: (l, s // tiles_per_seq, 0, 0))
    return pl.pallas_call(
        functools.partial(_mixer_prompt_kernel, l, tiles_per_seq),
        grid=(m // tm,),
        in_specs=[
            pl.BlockSpec(memory_space=pltpu.SMEM),
            pl.BlockSpec((tm, D_MODEL), lambda s: (s, 0)),
            _layer_vec(l, D_MODEL),
            _layer_vec(l, D_MODEL),
            _resident((D_MODEL, IN_WIDTH), lambda s: (0, 0)),
            _resident((MIX_WIDTH, D_MODEL), lambda s: (0, 0)),
            _resident(bias.shape, lambda s: (0, 0, 0, 0)),
            _resident((None,) + wpool.shape[1:], lambda s: (l, 0, 0, 0)),
            _layer_vec(l, POOL_WIDTH),
            pl.BlockSpec(memory_space=pl.ANY),
            pl.BlockSpec(memory_space=pl.ANY),
            pl.BlockSpec(memory_space=pl.ANY),
        ],
        out_specs=[
            pl.BlockSpec((tm, D_MODEL), lambda s: (s, 0)),
            state(WINDOW, KV_WIDTH),
            state(WINDOW, KV_WIDTH),
            state(POOL_STATE, POOL_WIDTH),
        ],
        out_shape=[
            jax.ShapeDtypeStruct((m, D_MODEL), F32),
            jax.ShapeDtypeStruct(kp.shape, F32),
            jax.ShapeDtypeStruct(vp.shape, F32),
            jax.ShapeDtypeStruct(pp.shape, F32),
        ],
        scratch_shapes=[
            pltpu.VMEM((tm, ATTN_WIDTH), F32),
            pltpu.VMEM((tm, 2 * KV_WIDTH), F32),
            pltpu.VMEM((tm, POOL_WIDTH), F32),
            pltpu.VMEM((tm, MIX_WIDTH), BF16),
            pltpu.VMEM((KV_PADS * N_KV_HEADS, WINDOW, LANES), BF16),
            pltpu.VMEM((POOL_PREV_ROWS, POOL_WIDTH), F32),
        ],
        input_output_aliases={9: 1, 10: 2, 11: 3},
        compiler_params=_params("arbitrary"),
        name="mixer_prompt",
    )(sinks, x, gpre, gpost, w_in_b, w_out_b, bias, wpool, pscale, kp, vp, pp)


def _inproj_sample_kernel(x_ref, g_ref, w_ref, proj_ref, wb_ref, h_ref):
    j = pl.program_id(0)

    @pl.when(j == 0)
    def _():
        h_ref[...] = _rms(x_ref[...], g_ref[...]).astype(BF16)

    wb = w_ref[...].astype(BF16)
    wb_ref[...] = wb
    p = jnp.dot(h_ref[...], wb, preferred_element_type=F32)
    proj_ref[...] = p * jnp.where(j < ATTN_WIDTH // WEIGHT_BLOCK, Q_SCALE, 1.0)


def _inproj_sample(l, x, g, w):
    m = x.shape[0]
    tn = WEIGHT_BLOCK
    return pl.pallas_call(
        _inproj_sample_kernel,
        grid=(IN_WIDTH // tn,),
        in_specs=[
            pl.BlockSpec((m, D_MODEL), lambda j: (0, 0)),
            _layer_vec(l, D_MODEL),
            pl.BlockSpec((None, D_MODEL, tn), lambda j: (l, 0, j)),
        ],
        out_specs=[
            pl.BlockSpec((m, tn), lambda j: (0, j)),
            pl.BlockSpec((D_MODEL, tn), lambda j: (0, j)),
        ],
        out_shape=[
            jax.ShapeDtypeStruct((m, IN_WIDTH), F32),
            jax.ShapeDtypeStruct((D_MODEL, IN_WIDTH), BF16),
        ],
        scratch_shapes=[pltpu.VMEM((m, D_MODEL), BF16)],
        compiler_params=_params("arbitrary"),
        name="inproj_sample",
    )(x, g, w)


def _shift_in(old, new, n_new):
    r = old.shape[0]
    rolled = pltpu.roll(old, r - n_new, axis=0)
    tail = pltpu.roll(new, SUBLANES - n_new, axis=0)
    row = lax.broadcasted_iota(jnp.int32, (SUBLANES, old.shape[1]), 0)
    last = jnp.where(row < SUBLANES - n_new, rolled[r - SUBLANES:], tail)
    return jnp.concatenate([rolled[:r - SUBLANES], last], axis=0)


def _mixer_sample_kernel(l, n_new, sinks_ref, proj_ref, ck_ref, cv_ref, st_ref, bias_ref, wpool_ref, pscale_ref,
                         ks_any, vs_any, ps_any, out_ref, ko_ref, vo_ref, po_ref):
    del ks_any, vs_any, ps_any
    rows = SAMPLE_ROWS
    pad = jnp.zeros((WINDOW - rows, KV_WIDTH), F32)
    pos = PAST_LEN + lax.broadcasted_iota(jnp.int32, (rows, 1), 0)
    seqs = range(SAMPLE_GROUP)
    k_new = [proj_ref[g, :, ATTN_WIDTH:ATTN_WIDTH + KV_WIDTH] for g in seqs]
    v_new = [proj_ref[g, :, ATTN_WIDTH + KV_WIDTH:ATTN_WIDTH + 2 * KV_WIDTH] for g in seqs]
    kk = [jnp.concatenate([ck_ref[g], k_new[g], pad], axis=0) for g in seqs]
    vv = [jnp.concatenate([cv_ref[g], v_new[g], pad], axis=0) for g in seqs]
    for kh in range(N_KV_HEADS):
        for g in seqs:
            _attention_head(kh, rows, lambda c: proj_ref[g, :, c:c + LANES], _kv_pads(kk[g], vv[g], kh),
                            lambda i: bias_ref[i], lambda hd: sinks_ref[l, hd], out_ref.at[g])
    for g in seqs:
        u_prev, u_cur = st_ref[g], proj_ref[g, :, ATTN_WIDTH + 2 * KV_WIDTH:]
        _pool(rows, u_prev, u_cur, pos, wpool_ref, pscale_ref, out_ref.at[g])
        ko_ref[g] = _shift_in(ck_ref[g], k_new[g], n_new)
        vo_ref[g] = _shift_in(cv_ref[g], v_new[g], n_new)
        po_ref[g] = _last_rows(_shift_in(u_prev, u_cur, n_new), POOL_STATE)


def _mixer_sample(l, proj, cache_k, cache_v, state, sinks, bias, wpool, pscale, ks, vs, ps, n_new):
    n_seq = cache_k.shape[1]
    rows = SAMPLE_ROWS
    grp = SAMPLE_GROUP
    assert n_seq % grp == 0
    per_seq = lambda r, width: pl.BlockSpec((None, grp, r, width), lambda n: (l, n, 0, 0))
    seqs = lambda width: pl.BlockSpec((grp, rows, width), lambda n: (n, 0, 0))
    return pl.pallas_call(
        functools.partial(_mixer_sample_kernel, l, n_new),
        grid=(n_seq // grp,),
        in_specs=[
            pl.BlockSpec(memory_space=pltpu.SMEM),
            seqs(IN_WIDTH),
            per_seq(WINDOW, KV_WIDTH),
            per_seq(WINDOW, KV_WIDTH),
            per_seq(POOL_PREV_ROWS, POOL_WIDTH),
            pl.BlockSpec(bias.shape, lambda n: (0, 0, 0)),
            pl.BlockSpec((None,) + wpool.shape[1:], lambda n: (l, 0, 0, 0)),
            _layer_vec(l, POOL_WIDTH),
            pl.BlockSpec(memory_space=pl.ANY),
            pl.BlockSpec(memory_space=pl.ANY),
            pl.BlockSpec(memory_space=pl.ANY),
        ],
        out_specs=[
            seqs(MIX_WIDTH),
            per_seq(WINDOW, KV_WIDTH),
            per_seq(WINDOW, KV_WIDTH),
            per_seq(POOL_STATE, POOL_WIDTH),
        ],
        out_shape=[
            jax.ShapeDtypeStruct((n_seq, rows, MIX_WIDTH), F32),
            jax.ShapeDtypeStruct(ks.shape, F32),
            jax.ShapeDtypeStruct(vs.shape, F32),
            jax.ShapeDtypeStruct(ps.shape, F32),
        ],
        input_output_aliases={8: 1, 9: 2, 10: 3},
        compiler_params=_params("arbitrary"),
        name="mixer_sample",
    )(sinks, proj.reshape(n_seq, rows, IN_WIDTH), cache_k, cache_v, state, bias, wpool, pscale, ks, vs, ps)


def _outproj_sample_kernel(mix_ref, x_ref, g_ref, w_ref, o_ref, wb_ref):
    k = pl.program_id(0)

    @pl.when(k == 0)
    def _():
        o_ref[...] = jnp.zeros_like(o_ref)

    wb = w_ref[...].astype(BF16)
    wb_ref[...] = wb
    o_ref[...] += jnp.dot(mix_ref[...].astype(BF16), wb, preferred_element_type=F32)

    @pl.when(k == pl.num_programs(0) - 1)
    def _():
        o_ref[...] = x_ref[...] + _rms(o_ref[...], g_ref[...])


def _outproj_sample(l, mix, x, g, w):
    m = x.shape[0]
    tk = WEIGHT_BLOCK
    return pl.pallas_call(
        _outproj_sample_kernel,
        grid=(MIX_WIDTH // tk,),
        in_specs=[
            pl.BlockSpec((m, tk), lambda k: (0, k)),
            pl.BlockSpec((m, D_MODEL), lambda k: (0, 0)),
            _layer_vec(l, D_MODEL),
            pl.BlockSpec((None, tk, D_MODEL), lambda k: (l, k, 0)),
        ],
        out_specs=[
            pl.BlockSpec((m, D_MODEL), lambda k: (0, 0)),
            pl.BlockSpec((tk, D_MODEL), lambda k: (k, 0)),
        ],
        out_shape=[
            jax.ShapeDtypeStruct((m, D_MODEL), F32),
            jax.ShapeDtypeStruct((MIX_WIDTH, D_MODEL), BF16),
        ],
        compiler_params=_params("arbitrary"),
        name="outproj_sample",
    )(mix, x, g, w)


def _rms_chunked(src_ref, inv_ref, finish):
    n = src_ref.shape[0]

    def stats(c, carry):
        rows = pl.ds(pl.multiple_of(c * STAT_ROWS, STAT_ROWS), STAT_ROWS)
        y = src_ref[rows, :]
        inv = lax.rsqrt(jnp.mean(y * y, axis=-1, keepdims=True) + EPS)
        inv_ref[rows, :] = jnp.broadcast_to(inv, (STAT_ROWS, LANES))
        return carry

    def apply(c, carry):
        rows = pl.ds(pl.multiple_of(c * APPLY_ROWS, APPLY_ROWS), APPLY_ROWS)
        inv = jnp.tile(inv_ref[rows, :], (1, src_ref.shape[1] // LANES))
        finish(rows, src_ref[rows, :] * inv)
        return carry

    lax.fori_loop(0, n // STAT_ROWS, stats, 0, unroll=True)
    lax.fori_loop(0, n // APPLY_ROWS, apply, 0)


def _ffn_kernel(cast_weights, x_ref, gpre_ref, gpost_ref, wup_ref, wdn_ref, o_ref, *rest):
    f = pl.program_id(1)
    h_ref, inv_ref = rest[-2:]

    @pl.when(f == 0)
    def _():
        g = gpre_ref[...]

        def store_h(rows, xn):
            h_ref[rows, :] = (xn * g).astype(BF16)
            o_ref[rows, :] = jnp.zeros((APPLY_ROWS, D_MODEL), F32)

        _rms_chunked(x_ref, inv_ref, store_h)

    if cast_weights:
        wupb_ref, wdnb_ref = rest[:2]
        wupb_ref[...] = wup_ref[...].astype(BF16)
        wdnb_ref[...] = wdn_ref[...].astype(BF16)
        wup_ref, wdn_ref = wupb_ref, wdnb_ref
    a = jnp.dot(h_ref[...], wup_ref[...], preferred_element_type=F32)
    a = jnp.square(jnp.maximum(a, 0.0)).astype(BF16)
    o_ref[...] += jnp.dot(a, wdn_ref[...], preferred_element_type=F32)

    @pl.when(f == pl.num_programs(1) - 1)
    def _():
        g = gpost_ref[...]

        def store_out(rows, yn):
            o_ref[rows, :] = x_ref[rows, :] + yn * g

        _rms_chunked(o_ref, inv_ref, store_out)


def _ffn(l, x, gpre, gpost, wup, wdn, tm, tf, cast_weights):
    m = x.shape[0]
    if cast_weights:
        assert m == tm
        w_specs = [pl.BlockSpec((None, D_MODEL, tf), lambda i, f: (l, 0, f)),
                   pl.BlockSpec((None, tf, D_MODEL), lambda i, f: (l, f, 0))]
        extra_specs = [pl.BlockSpec((D_MODEL, tf), lambda i, f: (0, f)),
                       pl.BlockSpec((tf, D_MODEL), lambda i, f: (f, 0))]
        extra_shapes = [jax.ShapeDtypeStruct((D_MODEL, D_FF), BF16), jax.ShapeDtypeStruct((D_FF, D_MODEL), BF16)]
    else:
        w_specs = [pl.BlockSpec((D_MODEL, tf), lambda i, f: (0, f)),
                   pl.BlockSpec((tf, D_MODEL), lambda i, f: (f, 0))]
        extra_specs, extra_shapes = [], []
    return pl.pallas_call(
        functools.partial(_ffn_kernel, cast_weights),
        grid=(m // tm, D_FF // tf),
        in_specs=[
            pl.BlockSpec((tm, D_MODEL), lambda i, f: (i, 0)),
            _layer_vec(l, D_MODEL),
            _layer_vec(l, D_MODEL),
        ] + w_specs,
        out_specs=[pl.BlockSpec((tm, D_MODEL), lambda i, f: (i, 0))] + extra_specs,
        out_shape=[jax.ShapeDtypeStruct((m, D_MODEL), F32)] + extra_shapes,
        scratch_shapes=[pltpu.VMEM((tm, D_MODEL), BF16), pltpu.VMEM((tm, LANES), F32)],
        compiler_params=_params("arbitrary", "arbitrary"),
        name="ffn_sample" if cast_weights else "ffn_prompt",
    )(x, gpre, gpost, wup, wdn)


def _bias_tables():
    heads = np.arange(1, N_Q_HEADS + 1, dtype=np.float32)
    slopes = np.exp2(np.float32(-8.0) * heads / np.float32(N_Q_HEADS)).astype(np.float32)

    def table(rows, mask_prev):
        i = np.arange(rows)[:, None]
        j = np.arange(2 * WINDOW)[None, :]
        dist = i + WINDOW - j
        valid = (dist >= 0) & (dist < WINDOW)
        if mask_prev:
            valid = valid & (j >= WINDOW)
        out = np.empty((2 * N_KV_HEADS, 2 * rows, 2 * WINDOW), np.float32)
        for kh in range(N_KV_HEADS):
            for par in range(2):
                for half, head in enumerate((4 * kh + par, 4 * kh + 2 + par)):
                    bias = (-slopes[head]) * dist.astype(np.float32)
                    out[2 * kh + par, half * rows:(half + 1) * rows] = np.where(valid, bias, np.float32(NEG_BIG))
        return out

    prompt = np.stack([table(WINDOW, False), table(WINDOW, True)])
    sample = table(SAMPLE_ROWS, False)
    return jnp.asarray(prompt), jnp.asarray(sample)


def kernel(x_prompt, x_sample, cache_k, cache_v, state_pool, w_in, w_out, w_pool, pool_scale, attn_sinks,
           g_pre_mix, g_post_mix, g_pre_ffn, g_post_ffn, w_up, w_down):
    n_seq, seq, _ = x_prompt.shape
    dec_batch, dec_seq, _ = x_sample.shape
    assert seq % MIX_TM == 0 and (n_seq * seq) % FFN_TM == 0 and dec_seq <= SAMPLE_ROWS
    bias_prompt, bias_sample = _bias_tables()

    vec = lambda p: p.reshape(DEPTH, 1, p.shape[-1])
    gpm, gqm, gpf, gqf, pscale = vec(g_pre_mix), vec(g_post_mix), vec(g_pre_ffn), vec(g_post_ffn), vec(pool_scale)
    w_pool_b = w_pool.astype(BF16)

    xp = x_prompt.reshape(n_seq * seq, D_MODEL)
    xs = jnp.pad(x_sample, ((0, 0), (0, SAMPLE_ROWS - dec_seq), (0, 0))).reshape(dec_batch * SAMPLE_ROWS, D_MODEL)
    ck_all = cache_k.reshape(DEPTH, dec_batch, WINDOW, KV_WIDTH)
    cv_all = cache_v.reshape(DEPTH, dec_batch, WINDOW, KV_WIDTH)
    st_all = jnp.pad(state_pool, ((0, 0), (0, 0), (1, 0), (0, 0)))

    kp = jnp.zeros((DEPTH, n_seq, WINDOW, KV_WIDTH), F32)
    vp = jnp.zeros((DEPTH, n_seq, WINDOW, KV_WIDTH), F32)
    pp = jnp.zeros((DEPTH, n_seq, POOL_STATE, POOL_WIDTH), F32)
    ks = jnp.zeros((DEPTH, dec_batch, WINDOW, KV_WIDTH), F32)
    vs = jnp.zeros((DEPTH, dec_batch, WINDOW, KV_WIDTH), F32)
    ps = jnp.zeros((DEPTH, dec_batch, POOL_STATE, POOL_WIDTH), F32)

    for l in range(DEPTH):
        proj, w_in_b = _inproj_sample(l, xs, gpm, w_in)
        mix, ks, vs, ps = _mixer_sample(l, proj, ck_all, cv_all, st_all, attn_sinks, bias_sample, w_pool_b, pscale,
                                        ks, vs, ps, dec_seq)
        xs, w_out_b = _outproj_sample(l, mix.reshape(xs.shape[0], MIX_WIDTH), xs, gqm, w_out)
        xs, w_up_b, w_down_b = _ffn(l, xs, gpf, gqf, w_up, w_down, xs.shape[0], WEIGHT_BLOCK, True)

        xp, kp, vp, pp = _mixer_prompt(l, xp, attn_sinks, gpm, gqm, w_in_b, w_out_b, bias_prompt, w_pool_b, pscale,
                                       kp, vp, pp, seq)
        (xp,) = _ffn(l, xp, gpf, gqf, w_up_b, w_down_b, FFN_TM, FFN_TF, False)

    y_prompt = xp.reshape(n_seq, seq, D_MODEL)
    y_sample = xs.reshape(dec_batch, SAMPLE_ROWS, D_MODEL)[:, :dec_seq]
    kv_shape = (WINDOW, N_KV_HEADS, HEAD_DIM)
    return (y_prompt, y_sample, kp.reshape((DEPTH, n_seq) + kv_shape), vp.reshape((DEPTH, n_seq) + kv_shape), pp,
            ks.reshape((DEPTH, dec_batch) + kv_shape), vs.reshape((DEPTH, dec_batch) + kv_shape), ps)
```

```python
import functools

import jax
import jax.numpy as jnp
import numpy as np
from jax import lax
from jax.experimental import pallas as pl
from jax.experimental.pallas import tpu as pltpu

D_MODEL = 2048
DEPTH = 4
PAST_LEN = 16384
HEAD_DIM = 64
N_Q_HEADS = 16
N_KV_HEADS = 4
ATTN_WIDTH = N_Q_HEADS * HEAD_DIM
KV_WIDTH = N_KV_HEADS * HEAD_DIM
WINDOW = 128
POOL_WINDOWS = (2, 4, 8, 16)
POOL_GROUP_WIDTH = 256
POOL_WIDTH = len(POOL_WINDOWS) * POOL_GROUP_WIDTH
POOL_STATE = max(POOL_WINDOWS) - 1
POOL_PREV_ROWS = POOL_STATE + 1
MIX_WIDTH = ATTN_WIDTH + POOL_WIDTH
IN_WIDTH = ATTN_WIDTH + 2 * KV_WIDTH + POOL_WIDTH
D_FF = 4 * D_MODEL
EPS = 1e-6
Q_SCALE = HEAD_DIM ** -0.5

LANES = 128
SUBLANES = 8
NEG_BIG = -1e30
VMEM_LIMIT_BYTES = 56 * 1024 * 1024
SAMPLE_ROWS = SUBLANES

FFN_TM = 512
FFN_TF = 1024
STAT_ROWS = 256
APPLY_ROWS = 64
SAMPLE_GROUP = 4
WEIGHT_BLOCK = 512
MIX_TM = 256
KV_PADS = 4

BF16 = jnp.bfloat16
F32 = jnp.float32


def _rms(x, g):
    ms = jnp.mean(x * x, axis=-1, keepdims=True)
    return (x * lax.rsqrt(ms + EPS)) * g


def _params(*semantics):
    return pltpu.CompilerParams(dimension_semantics=semantics, vmem_limit_bytes=VMEM_LIMIT_BYTES)


def _layer_vec(l, width):
    return pl.BlockSpec((None, 1, width), lambda *_: (l, 0, 0))


def _resident(shape, index_map):
    return pl.BlockSpec(shape, index_map, pipeline_mode=pl.Buffered(1))


def _last_rows(x, n):
    tail = x[x.shape[0] - 2 * SUBLANES:]
    return pltpu.roll(tail, n, axis=0)[:n]


def _half_padded(x2, head_in_high_half):
    lane = lax.broadcasted_iota(jnp.int32, x2.shape, 1)
    swapped = pltpu.roll(x2, HEAD_DIM, axis=1)
    in_lo, in_hi = (swapped, x2) if head_in_high_half else (x2, swapped)
    lo = jnp.where(lane < HEAD_DIM, in_lo, 0.0).astype(BF16)
    hi = jnp.where(lane >= HEAD_DIM, in_hi, 0.0).astype(BF16)
    return lo, hi


def _kv_pads(k, v, kh):
    col = (kh // 2) * LANES
    high = kh % 2 == 1
    return _half_padded(k[:, col:col + LANES], high) + _half_padded(v[:, col:col + LANES], high)


def _attention(items, rows, sink_at):
    keys = 2 * WINDOW
    row = lax.broadcasted_iota(jnp.int32, (2 * rows, 1), 0)
    lane = lax.broadcasted_iota(jnp.int32, (2 * rows, LANES), 1)

    scores = []
    for kh, q_at, (k_lo, k_hi, _, _), bias_at, _ in items:
        c0 = 2 * kh * LANES
        qq = jnp.concatenate([q_at(c0), q_at(c0 + LANES)], axis=0).astype(BF16)
        s = lax.dot_general(qq, jnp.concatenate([k_lo, k_hi], axis=0), (((1,), (1,)), ((), ())),
                            preferred_element_type=F32)
        scores.append([s[:, par * keys:(par + 1) * keys] + bias_at(2 * kh + par) for par in range(2)])

    sinks = [[jnp.where(row < rows, sink_at(4 * kh + par), sink_at(4 * kh + 2 + par)) for par in range(2)]
             for kh, *_ in items]
    maxes = [[jnp.maximum(jnp.max(s, axis=-1, keepdims=True), sk) for s, sk in zip(pair, sks)]
             for pair, sks in zip(scores, sinks)]
    probs = [[jnp.exp(s - m) for s, m in zip(pair, ms)] for pair, ms in zip(scores, maxes)]
    recips = [[1.0 / (jnp.sum(p, axis=-1, keepdims=True) + jnp.exp(sk - m)) for p, sk, m in zip(pair, sks, ms)]
              for pair, sks, ms in zip(probs, sinks, maxes)]

    for (kh, _, (_, _, v_lo, v_hi), _, out_ref), pair, rcp in zip(items, probs, recips):
        c0 = 2 * kh * LANES
        p_cat = jnp.concatenate([p.astype(BF16) for p in pair], axis=1)
        o = jnp.dot(p_cat, jnp.concatenate([v_lo, v_hi], axis=0), preferred_element_type=F32)
        o = o * jnp.where(lane < HEAD_DIM, rcp[0], rcp[1])
        out_ref[:, c0:c0 + LANES] = o[:rows].astype(out_ref.dtype)
        out_ref[:, c0 + LANES:c0 + 2 * LANES] = o[rows:].astype(out_ref.dtype)


def _pool(rows, u_prev, u_cur, pos, wpool_ref, pscale_ref, out_ref):
    ext = jnp.concatenate([u_prev, u_cur], axis=0)
    for gi, w in enumerate(POOL_WINDOWS):
        lo = gi * POOL_GROUP_WIDTH
        e = ext[:, lo:lo + POOL_GROUP_WIDTH]
        s, d = e, 1
        while d < w:
            s = s + pltpu.roll(s, d, axis=0)
            d *= 2
        cnt = jnp.minimum(pos + 1, w).astype(F32)
        z = s[POOL_PREV_ROWS:] / cnt - e[POOL_PREV_ROWS:]
        zz = jnp.dot(z.astype(BF16), wpool_ref[gi], preferred_element_type=F32)
        zz = zz * pscale_ref[:, lo:lo + POOL_GROUP_WIDTH]
        out_ref[:, ATTN_WIDTH + lo:ATTN_WIDTH + lo + POOL_GROUP_WIDTH] = zz.astype(out_ref.dtype)


def _mixer_prompt_kernel(l, tiles_per_seq, sinks_ref, x_ref, gpre_ref, gpost_ref, win_ref, wout_ref,
                         bias_ref, wpool_ref, pscale_ref, kp_any, vp_any, pp_any,
                         out_ref, kp_ref, vp_ref, pp_ref, q_s, kv_s, u_s, mix_s, pads_s, utail_s):
    del kp_any, vp_any, pp_any
    s = pl.program_id(0)
    tile = s % tiles_per_seq
    blocks = MIX_TM // WINDOW

    @pl.when(s == 0)
    def _():
        pads_s[...] = jnp.zeros_like(pads_s)
        utail_s[...] = jnp.zeros_like(utail_s)

    h = _rms(x_ref[...], gpre_ref[...]).astype(BF16)
    q_s[...] = jnp.dot(h, win_ref[:, :ATTN_WIDTH], preferred_element_type=F32) * Q_SCALE
    kv_s[...] = jnp.dot(h, win_ref[:, ATTN_WIDTH:ATTN_WIDTH + 2 * KV_WIDTH], preferred_element_type=F32)
    u_s[...] = jnp.dot(h, win_ref[:, ATTN_WIDTH + 2 * KV_WIDTH:], preferred_element_type=F32)

    seq_start = tile == 0
    first = seq_start.astype(jnp.int32)
    block_rows = [pl.ds(j * WINDOW, WINDOW) for j in range(blocks)]
    prev_pads = [pads_s[i] for i in range(KV_PADS * N_KV_HEADS)]
    items = []
    for kh in range(N_KV_HEADS):
        pads = prev_pads[KV_PADS * kh:KV_PADS * (kh + 1)]
        for j, rows in enumerate(block_rows):
            cur = _kv_pads(kv_s[rows, :KV_WIDTH], kv_s[rows, KV_WIDTH:], kh)
            both = [jnp.concatenate([p, c], axis=0) for p, c in zip(pads, cur)]
            bias_at = (lambda i: bias_ref[first, i]) if j == 0 else (lambda i: bias_ref[0, i])
            items.append((kh, functools.partial(lambda rows, c: q_s[rows, c:c + LANES], rows), both, bias_at,
                          mix_s.at[rows]))
            pads = cur
        for i, p in enumerate(pads):
            pads_s[KV_PADS * kh + i] = p
    _attention(items, WINDOW, lambda hd: sinks_ref[l, hd])

    u_prev = jnp.where(seq_start, 0.0, utail_s[...])
    for j, rows in enumerate(block_rows):
        u_cur = u_s[rows, :]
        pos = (tile * blocks + j) * WINDOW + lax.broadcasted_iota(jnp.int32, (WINDOW, 1), 0)
        _pool(WINDOW, u_prev, u_cur, pos, wpool_ref, pscale_ref, mix_s.at[rows])
        u_prev = u_cur[WINDOW - POOL_PREV_ROWS:]
    utail_s[...] = u_prev

    y = jnp.dot(mix_s[...], wout_ref[...], preferred_element_type=F32)
    out_ref[...] = x_ref[...] + _rms(y, gpost_ref[...])

    @pl.when(tile == tiles_per_seq - 1)
    def _():
        kp_ref[...] = kv_s[MIX_TM - WINDOW:, :KV_WIDTH]
        vp_ref[...] = kv_s[MIX_TM - WINDOW:, KV_WIDTH:]
        pp_ref[...] = _last_rows(u_s[MIX_TM - 2 * SUBLANES:, :], POOL_STATE)


def _mixer_prompt(l, x, sinks, gpre, gpost, w_in_b, w_out_b, bias, wpool, pscale, kp, vp, pp, seq):
    m = x.shape[0]
    tm = MIX_TM
    tiles_per_seq = seq // tm
    state = lambda rows, width: pl.BlockSpec((None, None, rows, width), lambda s: (l, s // tiles_per_seq, 0, 0))
    return pl.pallas_call(
        functools.partial(_mixer_prompt_kernel, l, tiles_per_seq),
        grid=(m // tm,),
        in_specs=[
            pl.BlockSpec(memory_space=pltpu.SMEM),
            pl.BlockSpec((tm, D_MODEL), lambda s: (s, 0)),
            _layer_vec(l, D_MODEL),
            _layer_vec(l, D_MODEL),
            _resident((D_MODEL, IN_WIDTH), lambda s: (0, 0)),
            _resident((MIX_WIDTH, D_MODEL), lambda s: (0, 0)),
            _resident(bias.shape, lambda s: (0, 0, 0, 0)),
            _resident((None,) + wpool.shape[1:], lambda s: (l, 0, 0, 0)),
            _layer_vec(l, POOL_WIDTH),
            pl.BlockSpec(memory_space=pl.ANY),
            pl.BlockSpec(memory_space=pl.ANY),
            pl.BlockSpec(memory_space=pl.ANY),
        ],
        out_specs=[
            pl.BlockSpec((tm, D_MODEL), lambda s: (s, 0)),
            state(WINDOW, KV_WIDTH),
            state(WINDOW, KV_WIDTH),
            state(POOL_STATE, POOL_WIDTH),
        ],
        out_shape=[
            jax.ShapeDtypeStruct((m, D_MODEL), F32),
            jax.ShapeDtypeStruct(kp.shape, F32),
            jax.ShapeDtypeStruct(vp.shape, F32),
            jax.ShapeDtypeStruct(pp.shape, F32),
        ],
        scratch_shapes=[
            pltpu.VMEM((tm, ATTN_WIDTH), F32),
            pltpu.VMEM((tm, 2 * KV_WIDTH), F32),
            pltpu.VMEM((tm, POOL_WIDTH), F32),
            pltpu.VMEM((tm, MIX_WIDTH), BF16),
            pltpu.VMEM((KV_PADS * N_KV_HEADS, WINDOW, LANES), BF16),
            pltpu.VMEM((POOL_PREV_ROWS, POOL_WIDTH), F32),
        ],
        input_output_aliases={9: 1, 10: 2, 11: 3},
        compiler_params=_params("arbitrary"),
        name="mixer_prompt",
    )(sinks, x, gpre, gpost, w_in_b, w_out_b, bias, wpool, pscale, kp, vp, pp)


def _inproj_sample_kernel(x_ref, g_ref, w_ref, proj_ref, wb_ref, h_ref):
    j = pl.program_id(0)

    @pl.when(j == 0)
    def _():
        h_ref[...] = _rms(x_ref[...], g_ref[...]).astype(BF16)

    wb = w_ref[...].astype(BF16)
    wb_ref[...] = wb
    p = jnp.dot(h_ref[...], wb, preferred_element_type=F32)
    proj_ref[...] = p * jnp.where(j < ATTN_WIDTH // WEIGHT_BLOCK, Q_SCALE, 1.0)


def _inproj_sample(l, x, g, w):
    m = x.shape[0]
    tn = WEIGHT_BLOCK
    return pl.pallas_call(
        _inproj_sample_kernel,
        grid=(IN_WIDTH // tn,),
        in_specs=[
            pl.BlockSpec((m, D_MODEL), lambda j: (0, 0)),
            _layer_vec(l, D_MODEL),
            pl.BlockSpec((None, D_MODEL, tn), lambda j: (l, 0, j)),
        ],
        out_specs=[
            pl.BlockSpec((m, tn), lambda j: (0, j)),
            pl.BlockSpec((D_MODEL, tn), lambda j: (0, j)),
        ],
        out_shape=[
            jax.ShapeDtypeStruct((m, IN_WIDTH), F32),
            jax.ShapeDtypeStruct((D_MODEL, IN_WIDTH), BF16),
        ],
        scratch_shapes=[pltpu.VMEM((m, D_MODEL), BF16)],
        compiler_params=_params("arbitrary"),
        name="inproj_sample",
    )(x, g, w)


def _shift_in(old, new, n_new):
    r = old.shape[0]
    rolled = pltpu.roll(old, r - n_new, axis=0)
    tail = pltpu.roll(new, SUBLANES - n_new, axis=0)
    row = lax.broadcasted_iota(jnp.int32, (SUBLANES, old.shape[1]), 0)
    last = jnp.where(row < SUBLANES - n_new, rolled[r - SUBLANES:], tail)
    return jnp.concatenate([rolled[:r - SUBLANES], last], axis=0)


def _mixer_sample_kernel(l, n_new, sinks_ref, proj_ref, ck_ref, cv_ref, st_ref, bias_ref, wpool_ref, pscale_ref,
                         ks_any, vs_any, ps_any, out_ref, ko_ref, vo_ref, po_ref):
    del ks_any, vs_any, ps_any
    rows = SAMPLE_ROWS
    pad = jnp.zeros((WINDOW - rows, KV_WIDTH), F32)
    pos = PAST_LEN + lax.broadcasted_iota(jnp.int32, (rows, 1), 0)
    seqs = range(SAMPLE_GROUP)
    k_new = [proj_ref[g, :, ATTN_WIDTH:ATTN_WIDTH + KV_WIDTH] for g in seqs]
    v_new = [proj_ref[g, :, ATTN_WIDTH + KV_WIDTH:ATTN_WIDTH + 2 * KV_WIDTH] for g in seqs]
    kk = [jnp.concatenate([ck_ref[g], k_new[g], pad], axis=0) for g in seqs]
    vv = [jnp.concatenate([cv_ref[g], v_new[g], pad], axis=0) for g in seqs]
    items = [(kh, functools.partial(lambda g, c: proj_ref[g, :, c:c + LANES], g), _kv_pads(kk[g], vv[g], kh),
              lambda i: bias_ref[i], out_ref.at[g]) for kh in range(N_KV_HEADS) for g in seqs]
    _attention(items, rows, lambda hd: sinks_ref[l, hd])
    for g in seqs:
        u_prev, u_cur = st_ref[g], proj_ref[g, :, ATTN_WIDTH + 2 * KV_WIDTH:]
        _pool(rows, u_prev, u_cur, pos, wpool_ref, pscale_ref, out_ref.at[g])
        ko_ref[g] = _shift_in(ck_ref[g], k_new[g], n_new)
        vo_ref[g] = _shift_in(cv_ref[g], v_new[g], n_new)
        po_ref[g] = _last_rows(_shift_in(u_prev, u_cur, n_new), POOL_STATE)


def _mixer_sample(l, proj, cache_k, cache_v, state, sinks, bias, wpool, pscale, ks, vs, ps, n_new):
    n_seq = cache_k.shape[1]
    rows = SAMPLE_ROWS
    grp = SAMPLE_GROUP
    assert n_seq % grp == 0
    per_seq = lambda r, width: pl.BlockSpec((None, grp, r, width), lambda n: (l, n, 0, 0))
    seqs = lambda width: pl.BlockSpec((grp, rows, width), lambda n: (n, 0, 0))
    return pl.pallas_call(
        functools.partial(_mixer_sample_kernel, l, n_new),
        grid=(n_seq // grp,),
        in_specs=[
            pl.BlockSpec(memory_space=pltpu.SMEM),
            seqs(IN_WIDTH),
            per_seq(WINDOW, KV_WIDTH),
            per_seq(WINDOW, KV_WIDTH),
            per_seq(POOL_PREV_ROWS, POOL_WIDTH),
            pl.BlockSpec(bias.shape, lambda n: (0, 0, 0)),
            pl.BlockSpec((None,) + wpool.shape[1:], lambda n: (l, 0, 0, 0)),
            _layer_vec(l, POOL_WIDTH),
            pl.BlockSpec(memory_space=pl.ANY),
            pl.BlockSpec(memory_space=pl.ANY),
            pl.BlockSpec(memory_space=pl.ANY),
        ],
        out_specs=[
            seqs(MIX_WIDTH),
            per_seq(WINDOW, KV_WIDTH),
            per_seq(WINDOW, KV_WIDTH),
            per_seq(POOL_STATE, POOL_WIDTH),
        ],
        out_shape=[
            jax.ShapeDtypeStruct((n_seq, rows, MIX_WIDTH), F32),
            jax.ShapeDtypeStruct(ks.shape, F32),
            jax.ShapeDtypeStruct(vs.shape, F32),
            jax.ShapeDtypeStruct(ps.shape, F32),
        ],
        input_output_aliases={8: 1, 9: 2, 10: 3},
        compiler_params=_params("arbitrary"),
        name="mixer_sample",
    )(sinks, proj.reshape(n_seq, rows, IN_WIDTH), cache_k, cache_v, state, bias, wpool, pscale, ks, vs, ps)


def _outproj_sample_kernel(mix_ref, x_ref, g_ref, w_ref, o_ref, wb_ref):
    k = pl.program_id(0)

    @pl.when(k == 0)
    def _():
        o_ref[...] = jnp.zeros_like(o_ref)

    wb = w_ref[...].astype(BF16)
    wb_ref[...] = wb
    o_ref[...] += jnp.dot(mix_ref[...].astype(BF16), wb, preferred_element_type=F32)

    @pl.when(k == pl.num_programs(0) - 1)
    def _():
        o_ref[...] = x_ref[...] + _rms(o_ref[...], g_ref[...])


def _outproj_sample(l, mix, x, g, w):
    m = x.shape[0]
    tk = WEIGHT_BLOCK
    return pl.pallas_call(
        _outproj_sample_kernel,
        grid=(MIX_WIDTH // tk,),
        in_specs=[
            pl.BlockSpec((m, tk), lambda k: (0, k)),
            pl.BlockSpec((m, D_MODEL), lambda k: (0, 0)),
            _layer_vec(l, D_MODEL),
            pl.BlockSpec((None, tk, D_MODEL), lambda k: (l, k, 0)),
        ],
        out_specs=[
            pl.BlockSpec((m, D_MODEL), lambda k: (0, 0)),
            pl.BlockSpec((tk, D_MODEL), lambda k: (k, 0)),
        ],
        out_shape=[
            jax.ShapeDtypeStruct((m, D_MODEL), F32),
            jax.ShapeDtypeStruct((MIX_WIDTH, D_MODEL), BF16),
        ],
        compiler_params=_params("arbitrary"),
        name="outproj_sample",
    )(mix, x, g, w)


def _rms_chunked(src_ref, inv_ref, finish):
    n = src_ref.shape[0]

    def stats(c, carry):
        rows = pl.ds(pl.multiple_of(c * STAT_ROWS, STAT_ROWS), STAT_ROWS)
        y = src_ref[rows, :]
        inv = lax.rsqrt(jnp.mean(y * y, axis=-1, keepdims=True) + EPS)
        inv_ref[rows, :] = jnp.broadcast_to(inv, (STAT_ROWS, LANES))
        return carry

    def apply(c, carry):
        rows = pl.ds(pl.multiple_of(c * APPLY_ROWS, APPLY_ROWS), APPLY_ROWS)
        inv = jnp.tile(inv_ref[rows, :], (1, src_ref.shape[1] // LANES))
        finish(rows, src_ref[rows, :] * inv)
        return carry

    lax.fori_loop(0, n // STAT_ROWS, stats, 0, unroll=True)
    lax.fori_loop(0, n // APPLY_ROWS, apply, 0)


def _ffn_kernel(cast_weights, x_ref, gpre_ref, gpost_ref, wup_ref, wdn_ref, o_ref, *rest):
    f = pl.program_id(1)
    h_ref, inv_ref = rest[-2:]

    @pl.when(f == 0)
    def _():
        g = gpre_ref[...]

        def store_h(rows, xn):
            h_ref[rows, :] = (xn * g).astype(BF16)
            o_ref[rows, :] = jnp.zeros((APPLY_ROWS, D_MODEL), F32)

        _rms_chunked(x_ref, inv_ref, store_h)

    if cast_weights:
        wupb_ref, wdnb_ref = rest[:2]
        wupb_ref[...] = wup_ref[...].astype(BF16)
        wdnb_ref[...] = wdn_ref[...].astype(BF16)
        wup_ref, wdn_ref = wupb_ref, wdnb_ref
    a = jnp.dot(h_ref[...], wup_ref[...], preferred_element_type=F32)
    a = jnp.square(jnp.maximum(a, 0.0)).astype(BF16)
    o_ref[...] += jnp.dot(a, wdn_ref[...], preferred_element_type=F32)

    @pl.when(f == pl.num_programs(1) - 1)
    def _():
        g = gpost_ref[...]

        def store_out(rows, yn):
            o_ref[rows, :] = x_ref[rows, :] + yn * g

        _rms_chunked(o_ref, inv_ref, store_out)


def _ffn(l, x, gpre, gpost, wup, wdn, tm, tf, cast_weights):
    m = x.shape[0]
    if cast_weights:
        assert m == tm
        w_specs = [pl.BlockSpec((None, D_MODEL, tf), lambda i, f: (l, 0, f)),
                   pl.BlockSpec((None, tf, D_MODEL), lambda i, f: (l, f, 0))]
        extra_specs = [pl.BlockSpec((D_MODEL, tf), lambda i, f: (0, f)),
                       pl.BlockSpec((tf, D_MODEL), lambda i, f: (f, 0))]
        extra_shapes = [jax.ShapeDtypeStruct((D_MODEL, D_FF), BF16), jax.ShapeDtypeStruct((D_FF, D_MODEL), BF16)]
    else:
        w_specs = [pl.BlockSpec((D_MODEL, tf), lambda i, f: (0, f)),
                   pl.BlockSpec((tf, D_MODEL), lambda i, f: (f, 0))]
        extra_specs, extra_shapes = [], []
    return pl.pallas_call(
        functools.partial(_ffn_kernel, cast_weights),
        grid=(m // tm, D_FF // tf),
        in_specs=[
            pl.BlockSpec((tm, D_MODEL), lambda i, f: (i, 0)),
            _layer_vec(l, D_MODEL),
            _layer_vec(l, D_MODEL),
        ] + w_specs,
        out_specs=[pl.BlockSpec((tm, D_MODEL), lambda i, f: (i, 0))] + extra_specs,
        out_shape=[jax.ShapeDtypeStruct((m, D_MODEL), F32)] + extra_shapes,
        scratch_shapes=[pltpu.VMEM((tm, D_MODEL), BF16), pltpu.VMEM((tm, LANES), F32)],
        compiler_params=_params("arbitrary", "arbitrary"),
        name="ffn_sample" if cast_weights else "ffn_prompt",
    )(x, gpre, gpost, wup, wdn)


def _bias_tables():
    heads = np.arange(1, N_Q_HEADS + 1, dtype=np.float32)
    slopes = np.exp2(np.float32(-8.0) * heads / np.float32(N_Q_HEADS)).astype(np.float32)

    def table(rows, mask_prev):
        i = np.arange(rows)[:, None]
        j = np.arange(2 * WINDOW)[None, :]
        dist = i + WINDOW - j
        valid = (dist >= 0) & (dist < WINDOW)
        if mask_prev:
            valid = valid & (j >= WINDOW)
        out = np.empty((2 * N_KV_HEADS, 2 * rows, 2 * WINDOW), np.float32)
        for kh in range(N_KV_HEADS):
            for par in range(2):
                for half, head in enumerate((4 * kh + par, 4 * kh + 2 + par)):
                    bias = (-slopes[head]) * dist.astype(np.float32)
                    out[2 * kh + par, half * rows:(half + 1) * rows] = np.where(valid, bias, np.float32(NEG_BIG))
        return out

    prompt = np.stack([table(WINDOW, False), table(WINDOW, True)])
    sample = table(SAMPLE_ROWS, False)
    return jnp.asarray(prompt), jnp.asarray(sample)


def kernel(x_prompt, x_sample, cache_k, cache_v, state_pool, w_in, w_out, w_pool, pool_scale, attn_sinks,
           g_pre_mix, g_post_mix, g_pre_ffn, g_post_ffn, w_up, w_down):
    n_seq, seq, _ = x_prompt.shape
    dec_batch, dec_seq, _ = x_sample.shape
    assert seq % MIX_TM == 0 and (n_seq * seq) % FFN_TM == 0 and dec_seq <= SAMPLE_ROWS
    bias_prompt, bias_sample = _bias_tables()

    vec = lambda p: p.reshape(DEPTH, 1, p.shape[-1])
    gpm, gqm, gpf, gqf, pscale = vec(g_pre_mix), vec(g_post_mix), vec(g_pre_ffn), vec(g_post_ffn), vec(pool_scale)
    w_pool_b = w_pool.astype(BF16)

    xp = x_prompt.reshape(n_seq * seq, D_MODEL)
    xs = jnp.pad(x_sample, ((0, 0), (0, SAMPLE_ROWS - dec_seq), (0, 0))).reshape(dec_batch * SAMPLE_ROWS, D_MODEL)
    ck_all = cache_k.reshape(DEPTH, dec_batch, WINDOW, KV_WIDTH)
    cv_all = cache_v.reshape(DEPTH, dec_batch, WINDOW, KV_WIDTH)
    st_all = jnp.pad(state_pool, ((0, 0), (0, 0), (1, 0), (0, 0)))

    kp = jnp.zeros((DEPTH, n_seq, WINDOW, KV_WIDTH), F32)
    vp = jnp.zeros((DEPTH, n_seq, WINDOW, KV_WIDTH), F32)
    pp = jnp.zeros((DEPTH, n_seq, POOL_STATE, POOL_WIDTH), F32)
    ks = jnp.zeros((DEPTH, dec_batch, WINDOW, KV_WIDTH), F32)
    vs = jnp.zeros((DEPTH, dec_batch, WINDOW, KV_WIDTH), F32)
    ps = jnp.zeros((DEPTH, dec_batch, POOL_STATE, POOL_WIDTH), F32)

    for l in range(DEPTH):
        proj, w_in_b = _inproj_sample(l, xs, gpm, w_in)
        mix, ks, vs, ps = _mixer_sample(l, proj, ck_all, cv_all, st_all, attn_sinks, bias_sample, w_pool_b, pscale,
                                        ks, vs, ps, dec_seq)
        xs, w_out_b = _outproj_sample(l, mix.reshape(xs.shape[0], MIX_WIDTH), xs, gqm, w_out)
        xs, w_up_b, w_down_b = _ffn(l, xs, gpf, gqf, w_up, w_down, xs.shape[0], WEIGHT_BLOCK, True)

        xp, kp, vp, pp = _mixer_prompt(l, xp, attn_sinks, gpm, gqm, w_in_b, w_out_b, bias_prompt, w_pool_b, pscale,
                                       kp, vp, pp, seq)
        (xp,) = _ffn(l, xp, gpf, gqf, w_up_b, w_down_b, FFN_TM, FFN_TF, False)

    y_prompt = xp.reshape(n_seq, seq, D_MODEL)
    y_sample = xs.reshape(dec_batch, SAMPLE_ROWS, D_MODEL)[:, :dec_seq]
    kv_shape = (WINDOW, N_KV_HEADS, HEAD_DIM)
    return (y_prompt, y_sample, kp.reshape((DEPTH, n_seq) + kv_shape), vp.reshape((DEPTH, n_seq) + kv_shape), pp,
            ks.reshape((DEPTH, dec_batch) + kv_shape), vs.reshape((DEPTH, dec_batch) + kv_shape), ps)
```

```python
import functools

import jax
import jax.numpy as jnp
import numpy as np
from jax import lax
from jax.experimental import pallas as pl
from jax.experimental.pallas import tpu as pltpu

D_MODEL = 2048
DEPTH = 4
PAST_LEN = 16384
HEAD_DIM = 64
N_Q_HEADS = 16
N_KV_HEADS = 4
ATTN_WIDTH = N_Q_HEADS * HEAD_DIM
KV_WIDTH = N_KV_HEADS * HEAD_DIM
WINDOW = 128
POOL_WINDOWS = (2, 4, 8, 16)
POOL_GROUP_WIDTH = 256
POOL_WIDTH = len(POOL_WINDOWS) * POOL_GROUP_WIDTH
POOL_STATE = max(POOL_WINDOWS) - 1
POOL_PREV_ROWS = POOL_STATE + 1
MIX_WIDTH = ATTN_WIDTH + POOL_WIDTH
IN_WIDTH = ATTN_WIDTH + 2 * KV_WIDTH + POOL_WIDTH
D_FF = 4 * D_MODEL
EPS = 1e-6
Q_SCALE = HEAD_DIM ** -0.5

LANES = 128
SUBLANES = 8
NEG_BIG = -1e30
VMEM_LIMIT_BYTES = 56 * 1024 * 1024
SAMPLE_ROWS = SUBLANES

FFN_TM = 512
FFN_TF = 1024
STAT_ROWS = 256
APPLY_ROWS = 64
SAMPLE_GROUP = 4
WEIGHT_BLOCK = 512
MIX_TM = 256
MXU_CHUNK = 256
MIX_LAG = 2
KV_PADS = 4

BF16 = jnp.bfloat16
F32 = jnp.float32


def _rms(x, g):
    ms = jnp.mean(x * x, axis=-1, keepdims=True)
    return (x * lax.rsqrt(ms + EPS)) * g


def _params(*semantics):
    return pltpu.CompilerParams(dimension_semantics=semantics, vmem_limit_bytes=VMEM_LIMIT_BYTES)


def _layer_vec(l, width):
    return pl.BlockSpec((None, 1, width), lambda *_: (l, 0, 0))


def _resident(shape, index_map):
    return pl.BlockSpec(shape, index_map, pipeline_mode=pl.Buffered(1))


def _last_rows(x, n):
    tail = x[x.shape[0] - 2 * SUBLANES:]
    return pltpu.roll(tail, n, axis=0)[:n]


def _half_padded(x2, head_in_high_half):
    lane = lax.broadcasted_iota(jnp.int32, x2.shape, 1)
    swapped = pltpu.roll(x2, HEAD_DIM, axis=1)
    in_lo, in_hi = (swapped, x2) if head_in_high_half else (x2, swapped)
    lo = jnp.where(lane < HEAD_DIM, in_lo, 0.0).astype(BF16)
    hi = jnp.where(lane >= HEAD_DIM, in_hi, 0.0).astype(BF16)
    return lo, hi


def _kv_pads(k, v, kh):
    col = (kh // 2) * LANES
    high = kh % 2 == 1
    return _half_padded(k[:, col:col + LANES], high) + _half_padded(v[:, col:col + LANES], high)


def _attention(items, rows, sink_at, between=lambda: None):
    keys = 2 * WINDOW
    row = lax.broadcasted_iota(jnp.int32, (2 * rows, 1), 0)
    lane = lax.broadcasted_iota(jnp.int32, (2 * rows, LANES), 1)

    scores = []
    for kh, q_at, (k_lo, k_hi, _, _), bias_at, _ in items:
        c0 = 2 * kh * LANES
        qq = jnp.concatenate([q_at(c0), q_at(c0 + LANES)], axis=0).astype(BF16)
        s = lax.dot_general(qq, jnp.concatenate([k_lo, k_hi], axis=0), (((1,), (1,)), ((), ())),
                            preferred_element_type=F32)
        scores.append([s[:, par * keys:(par + 1) * keys] + bias_at(2 * kh + par) for par in range(2)])
        between()

    sinks = [[jnp.where(row < rows, sink_at(4 * kh + par), sink_at(4 * kh + 2 + par)) for par in range(2)]
             for kh, *_ in items]
    maxes = []
    for pair, sks in zip(scores, sinks):
        maxes.append([jnp.maximum(jnp.max(s, axis=-1, keepdims=True), sk) for s, sk in zip(pair, sks)])
        between()
    probs, recips = [], []
    for pair, sks, ms in zip(scores, sinks, maxes):
        probs.append([jnp.exp(s - m) for s, m in zip(pair, ms)])
        recips.append([1.0 / (jnp.sum(p, axis=-1, keepdims=True) + jnp.exp(sk - m))
                       for p, sk, m in zip(probs[-1], sks, ms)])
        between()

    for (kh, _, (_, _, v_lo, v_hi), _, out_ref), pair, rcp in zip(items, probs, recips):
        c0 = 2 * kh * LANES
        p_cat = jnp.concatenate([p.astype(BF16) for p in pair], axis=1)
        o = jnp.dot(p_cat, jnp.concatenate([v_lo, v_hi], axis=0), preferred_element_type=F32)
        o = o * jnp.where(lane < HEAD_DIM, rcp[0], rcp[1])
        out_ref[:, c0:c0 + LANES] = o[:rows].astype(out_ref.dtype)
        out_ref[:, c0 + LANES:c0 + 2 * LANES] = o[rows:].astype(out_ref.dtype)
        between()


def _pool(rows, u_prev, u_cur, pos, wpool_ref, pscale_ref, out_ref):
    ext = jnp.concatenate([u_prev, u_cur], axis=0)
    for gi, w in enumerate(POOL_WINDOWS):
        lo = gi * POOL_GROUP_WIDTH
        e = ext[:, lo:lo + POOL_GROUP_WIDTH]
        s, d = e, 1
        while d < w:
            s = s + pltpu.roll(s, d, axis=0)
            d *= 2
        cnt = jnp.minimum(pos + 1, w).astype(F32)
        z = s[POOL_PREV_ROWS:] / cnt - e[POOL_PREV_ROWS:]
        zz = jnp.dot(z.astype(BF16), wpool_ref[gi], preferred_element_type=F32)
        zz = zz * pscale_ref[:, lo:lo + POOL_GROUP_WIDTH]
        out_ref[:, ATTN_WIDTH + lo:ATTN_WIDTH + lo + POOL_GROUP_WIDTH] = zz.astype(out_ref.dtype)


def _mixer_prompt_kernel(l, tiles_per_seq, n_tiles, sinks_ref, x_ref, xo_ref, gpre_ref, gpost_ref, win_ref, wout_ref,
                         bias_ref, wpool_ref, pscale_ref, kp_any, vp_any, pp_any,
                         out_ref, kp_ref, vp_ref, pp_ref, *scratch):
    del kp_any, vp_any, pp_any
    q_s, kv_s, u_s, mix_s = scratch[0:2], scratch[2:4], scratch[4:6], scratch[6:8]
    y_s, pads_s, utail_s = scratch[8:]
    s = pl.program_id(0)
    blocks = MIX_TM // WINDOW
    block_rows = [pl.ds(j * WINDOW, WINDOW) for j in range(blocks)]

    @pl.when(s == 0)
    def _():
        pads_s[...] = jnp.zeros_like(pads_s)
        utail_s[...] = jnp.zeros_like(utail_s)

    def step(new, project_in, attend, project_out):
        old = 1 - new
        if project_in:
            h = _rms(x_ref[...], gpre_ref[...]).astype(BF16)

        def in_projection_chunk(c):
            r = jnp.dot(h, win_ref[:, c:c + MXU_CHUNK], preferred_element_type=F32)
            if c < ATTN_WIDTH:
                q_s[new][:, c:c + MXU_CHUNK] = r * Q_SCALE
            elif c < ATTN_WIDTH + 2 * KV_WIDTH:
                kv_s[new][:, c - ATTN_WIDTH:c - ATTN_WIDTH + MXU_CHUNK] = r
            else:
                c -= ATTN_WIDTH + 2 * KV_WIDTH
                u_s[new][:, c:c + MXU_CHUNK] = r

        def out_projection_chunk(c):
            y_s[:, c:c + MXU_CHUNK] = jnp.dot(mix_s[new][...], wout_ref[:, c:c + MXU_CHUNK],
                                              preferred_element_type=F32)

        def out_projection_finish():
            out_ref[...] = xo_ref[...] + _rms(y_s[...], gpost_ref[...])

        pending = []
        if project_out:
            pending += [functools.partial(out_projection_chunk, c) for c in range(0, D_MODEL, MXU_CHUNK)]
            pending += [out_projection_finish]
        if project_in:
            pending += [functools.partial(in_projection_chunk, c) for c in range(0, IN_WIDTH, MXU_CHUNK)]
        n_pending = len(pending)
        n_slots = 4 * N_KV_HEADS * blocks
        slots = [0]

        def between():
            slots[0] += 1
            while n_pending - len(pending) < slots[0] * n_pending // n_slots:
                pending.pop(0)()

        if not attend:
            while pending:
                pending.pop(0)()
            if project_in:
                state_outputs(new)
            return

        tile = (s - 1) % tiles_per_seq
        seq_start = tile == 0
        first = seq_start.astype(jnp.int32)
        prev_pads = [pads_s[i] for i in range(KV_PADS * N_KV_HEADS)]
        items = []
        for kh in range(N_KV_HEADS):
            pads = prev_pads[KV_PADS * kh:KV_PADS * (kh + 1)]
            for j, rows in enumerate(block_rows):
                cur = _kv_pads(kv_s[old][rows, :KV_WIDTH], kv_s[old][rows, KV_WIDTH:], kh)
                both = [jnp.concatenate([p, c], axis=0) for p, c in zip(pads, cur)]
                bias_at = (lambda i: bias_ref[first, i]) if j == 0 else (lambda i: bias_ref[0, i])
                items.append((kh, functools.partial(lambda rows, c: q_s[old][rows, c:c + LANES], rows), both,
                              bias_at, mix_s[old].at[rows]))
                pads = cur
            for i, p in enumerate(pads):
                pads_s[KV_PADS * kh + i] = p
        _attention(items, WINDOW, lambda hd: sinks_ref[l, hd], between)
        assert not pending

        u_prev = jnp.where(seq_start, 0.0, utail_s[...])
        for j, rows in enumerate(block_rows):
            u_cur = u_s[old][rows, :]
            pos = (tile * blocks + j) * WINDOW + lax.broadcasted_iota(jnp.int32, (WINDOW, 1), 0)
            _pool(WINDOW, u_prev, u_cur, pos, wpool_ref, pscale_ref, mix_s[old].at[rows])
            u_prev = u_cur[WINDOW - POOL_PREV_ROWS:]
        utail_s[...] = u_prev
        if project_in:
            state_outputs(new)

    def state_outputs(new):
        @pl.when(s % tiles_per_seq == tiles_per_seq - 1)
        def _():
            kp_ref[...] = kv_s[new][MIX_TM - WINDOW:, :KV_WIDTH]
            vp_ref[...] = kv_s[new][MIX_TM - WINDOW:, KV_WIDTH:]
            pp_ref[...] = _last_rows(u_s[new][MIX_TM - 2 * SUBLANES:, :], POOL_STATE)

    pl.when(s == 0)(functools.partial(step, 0, True, False, False))
    pl.when(s == 1)(functools.partial(step, 1, True, True, False))
    for new in range(2):
        pl.when((s >= MIX_LAG) & (s < n_tiles) & (s % 2 == new))(functools.partial(step, new, True, True, True))
    pl.when(s == n_tiles)(functools.partial(step, 0, False, True, True))
    pl.when(s == n_tiles + 1)(functools.partial(step, 1, False, False, True))


def _mixer_prompt(l, x, sinks, gpre, gpost, w_in_b, w_out_b, bias, wpool, pscale, kp, vp, pp, seq):
    m = x.shape[0]
    tm = MIX_TM
    tiles_per_seq = seq // tm
    n_tiles = m // tm
    assert n_tiles % 2 == 0 and n_tiles > MIX_LAG
    cur = lambda s: (jnp.minimum(s, n_tiles - 1), 0)
    lagging = lambda s: (jnp.maximum(s - MIX_LAG, 0), 0)
    state = lambda rows, width: pl.BlockSpec((None, None, rows, width),
                                             lambda s: (l, jnp.minimum(s, n_tiles - 1) // tiles_per_seq, 0, 0))
    return pl.pallas_call(
        functools.partial(_mixer_prompt_kernel, l, tiles_per_seq, n_tiles),
        grid=(n_tiles + MIX_LAG,),
        in_specs=[
            pl.BlockSpec(memory_space=pltpu.SMEM),
            pl.BlockSpec((tm, D_MODEL), cur),
            pl.BlockSpec((tm, D_MODEL), lagging),
            _layer_vec(l, D_MODEL),
            _layer_vec(l, D_MODEL),
            _resident((D_MODEL, IN_WIDTH), lambda s: (0, 0)),
            _resident((MIX_WIDTH, D_MODEL), lambda s: (0, 0)),
            _resident(bias.shape, lambda s: (0, 0, 0, 0)),
            _resident((None,) + wpool.shape[1:], lambda s: (l, 0, 0, 0)),
            _layer_vec(l, POOL_WIDTH),
            pl.BlockSpec(memory_space=pl.ANY),
            pl.BlockSpec(memory_space=pl.ANY),
            pl.BlockSpec(memory_space=pl.ANY),
        ],
        out_specs=[
            pl.BlockSpec((tm, D_MODEL), lagging),
            state(WINDOW, KV_WIDTH),
            state(WINDOW, KV_WIDTH),
            state(POOL_STATE, POOL_WIDTH),
        ],
        out_shape=[
            jax.ShapeDtypeStruct((m, D_MODEL), F32),
            jax.ShapeDtypeStruct(kp.shape, F32),
            jax.ShapeDtypeStruct(vp.shape, F32),
            jax.ShapeDtypeStruct(pp.shape, F32),
        ],
        scratch_shapes=[
            pltpu.VMEM((tm, ATTN_WIDTH), F32), pltpu.VMEM((tm, ATTN_WIDTH), F32),
            pltpu.VMEM((tm, 2 * KV_WIDTH), F32), pltpu.VMEM((tm, 2 * KV_WIDTH), F32),
            pltpu.VMEM((tm, POOL_WIDTH), F32), pltpu.VMEM((tm, POOL_WIDTH), F32),
            pltpu.VMEM((tm, MIX_WIDTH), BF16), pltpu.VMEM((tm, MIX_WIDTH), BF16),
            pltpu.VMEM((tm, D_MODEL), F32),
            pltpu.VMEM((KV_PADS * N_KV_HEADS, WINDOW, LANES), BF16),
            pltpu.VMEM((POOL_PREV_ROWS, POOL_WIDTH), F32),
        ],
        input_output_aliases={10: 1, 11: 2, 12: 3},
        compiler_params=_params("arbitrary"),
        name="mixer_prompt",
    )(sinks, x, x, gpre, gpost, w_in_b, w_out_b, bias, wpool, pscale, kp, vp, pp)


def _inproj_sample_kernel(x_ref, g_ref, w_ref, proj_ref, wb_ref, h_ref):
    j = pl.program_id(0)

    @pl.when(j == 0)
    def _():
        h_ref[...] = _rms(x_ref[...], g_ref[...]).astype(BF16)

    wb = w_ref[...].astype(BF16)
    wb_ref[...] = wb
    p = jnp.dot(h_ref[...], wb, preferred_element_type=F32)
    proj_ref[...] = p * jnp.where(j < ATTN_WIDTH // WEIGHT_BLOCK, Q_SCALE, 1.0)


def _inproj_sample(l, x, g, w):
    m = x.shape[0]
    tn = WEIGHT_BLOCK
    return pl.pallas_call(
        _inproj_sample_kernel,
        grid=(IN_WIDTH // tn,),
        in_specs=[
            pl.BlockSpec((m, D_MODEL), lambda j: (0, 0)),
            _layer_vec(l, D_MODEL),
            pl.BlockSpec((None, D_MODEL, tn), lambda j: (l, 0, j)),
        ],
        out_specs=[
            pl.BlockSpec((m, tn), lambda j: (0, j)),
            pl.BlockSpec((D_MODEL, tn), lambda j: (0, j)),
        ],
        out_shape=[
            jax.ShapeDtypeStruct((m, IN_WIDTH), F32),
            jax.ShapeDtypeStruct((D_MODEL, IN_WIDTH), BF16),
        ],
        scratch_shapes=[pltpu.VMEM((m, D_MODEL), BF16)],
        compiler_params=_params("arbitrary"),
        name="inproj_sample",
    )(x, g, w)


def _shift_in(old, new, n_new):
    r = old.shape[0]
    rolled = pltpu.roll(old, r - n_new, axis=0)
    tail = pltpu.roll(new, SUBLANES - n_new, axis=0)
    row = lax.broadcasted_iota(jnp.int32, (SUBLANES, old.shape[1]), 0)
    last = jnp.where(row < SUBLANES - n_new, rolled[r - SUBLANES:], tail)
    return jnp.concatenate([rolled[:r - SUBLANES], last], axis=0)


def _mixer_sample_kernel(l, n_new, sinks_ref, proj_ref, ck_ref, cv_ref, st_ref, bias_ref, wpool_ref, pscale_ref,
                         ks_any, vs_any, ps_any, out_ref, ko_ref, vo_ref, po_ref):
    del ks_any, vs_any, ps_any
    rows = SAMPLE_ROWS
    pad = jnp.zeros((WINDOW - rows, KV_WIDTH), F32)
    pos = PAST_LEN + lax.broadcasted_iota(jnp.int32, (rows, 1), 0)
    seqs = range(SAMPLE_GROUP)
    k_new = [proj_ref[g, :, ATTN_WIDTH:ATTN_WIDTH + KV_WIDTH] for g in seqs]
    v_new = [proj_ref[g, :, ATTN_WIDTH + KV_WIDTH:ATTN_WIDTH + 2 * KV_WIDTH] for g in seqs]
    kk = [jnp.concatenate([ck_ref[g], k_new[g], pad], axis=0) for g in seqs]
    vv = [jnp.concatenate([cv_ref[g], v_new[g], pad], axis=0) for g in seqs]
    items = [(kh, functools.partial(lambda g, c: proj_ref[g, :, c:c + LANES], g), _kv_pads(kk[g], vv[g], kh),
              lambda i: bias_ref[i], out_ref.at[g]) for kh in range(N_KV_HEADS) for g in seqs]
    _attention(items, rows, lambda hd: sinks_ref[l, hd])
    for g in seqs:
        u_prev, u_cur = st_ref[g], proj_ref[g, :, ATTN_WIDTH + 2 * KV_WIDTH:]
        _pool(rows, u_prev, u_cur, pos, wpool_ref, pscale_ref, out_ref.at[g])
        ko_ref[g] = _shift_in(ck_ref[g], k_new[g], n_new)
        vo_ref[g] = _shift_in(cv_ref[g], v_new[g], n_new)
        po_ref[g] = _last_rows(_shift_in(u_prev, u_cur, n_new), POOL_STATE)


def _mixer_sample(l, proj, cache_k, cache_v, state, sinks, bias, wpool, pscale, ks, vs, ps, n_new):
    n_seq = cache_k.shape[1]
    rows = SAMPLE_ROWS
    grp = SAMPLE_GROUP
    assert n_seq % grp == 0
    per_seq = lambda r, width: pl.BlockSpec((None, grp, r, width), lambda n: (l, n, 0, 0))
    seqs = lambda width: pl.BlockSpec((grp, rows, width), lambda n: (n, 0, 0))
    return pl.pallas_call(
        functools.partial(_mixer_sample_kernel, l, n_new),
        grid=(n_seq // grp,),
        in_specs=[
            pl.BlockSpec(memory_space=pltpu.SMEM),
            seqs(IN_WIDTH),
            per_seq(WINDOW, KV_WIDTH),
            per_seq(WINDOW, KV_WIDTH),
            per_seq(POOL_PREV_ROWS, POOL_WIDTH),
            pl.BlockSpec(bias.shape, lambda n: (0, 0, 0)),
            pl.BlockSpec((None,) + wpool.shape[1:], lambda n: (l, 0, 0, 0)),
            _layer_vec(l, POOL_WIDTH),
            pl.BlockSpec(memory_space=pl.ANY),
            pl.BlockSpec(memory_space=pl.ANY),
            pl.BlockSpec(memory_space=pl.ANY),
        ],
        out_specs=[
            seqs(MIX_WIDTH),
            per_seq(WINDOW, KV_WIDTH),
            per_seq(WINDOW, KV_WIDTH),
            per_seq(POOL_STATE, POOL_WIDTH),
        ],
        out_shape=[
            jax.ShapeDtypeStruct((n_seq, rows, MIX_WIDTH), F32),
            jax.ShapeDtypeStruct(ks.shape, F32),
            jax.ShapeDtypeStruct(vs.shape, F32),
            jax.ShapeDtypeStruct(ps.shape, F32),
        ],
        input_output_aliases={8: 1, 9: 2, 10: 3},
        compiler_params=_params("arbitrary"),
        name="mixer_sample",
    )(sinks, proj.reshape(n_seq, rows, IN_WIDTH), cache_k, cache_v, state, bias, wpool, pscale, ks, vs, ps)


def _outproj_sample_kernel(mix_ref, x_ref, g_ref, w_ref, o_ref, wb_ref):
    k = pl.program_id(0)

    @pl.when(k == 0)
    def _():
        o_ref[...] = jnp.zeros_like(o_ref)

    wb = w_ref[...].astype(BF16)
    wb_ref[...] = wb
    o_ref[...] += jnp.dot(mix_ref[...].astype(BF16), wb, preferred_element_type=F32)

    @pl.when(k == pl.num_programs(0) - 1)
    def _():
        o_ref[...] = x_ref[...] + _rms(o_ref[...], g_ref[...])


def _outproj_sample(l, mix, x, g, w):
    m = x.shape[0]
    tk = WEIGHT_BLOCK
    return pl.pallas_call(
        _outproj_sample_kernel,
        grid=(MIX_WIDTH // tk,),
        in_specs=[
            pl.BlockSpec((m, tk), lambda k: (0, k)),
            pl.BlockSpec((m, D_MODEL), lambda k: (0, 0)),
            _layer_vec(l, D_MODEL),
            pl.BlockSpec((None, tk, D_MODEL), lambda k: (l, k, 0)),
        ],
        out_specs=[
            pl.BlockSpec((m, D_MODEL), lambda k: (0, 0)),
            pl.BlockSpec((tk, D_MODEL), lambda k: (k, 0)),
        ],
        out_shape=[
            jax.ShapeDtypeStruct((m, D_MODEL), F32),
            jax.ShapeDtypeStruct((MIX_WIDTH, D_MODEL), BF16),
        ],
        compiler_params=_params("arbitrary"),
        name="outproj_sample",
    )(mix, x, g, w)


def _rms_chunked(src_ref, inv_ref, finish):
    n = src_ref.shape[0]

    def stats(c, carry):
        rows = pl.ds(pl.multiple_of(c * STAT_ROWS, STAT_ROWS), STAT_ROWS)
        y = src_ref[rows, :]
        inv = lax.rsqrt(jnp.mean(y * y, axis=-1, keepdims=True) + EPS)
        inv_ref[rows, :] = jnp.broadcast_to(inv, (STAT_ROWS, LANES))
        return carry

    def apply(c, carry):
        rows = pl.ds(pl.multiple_of(c * APPLY_ROWS, APPLY_ROWS), APPLY_ROWS)
        inv = jnp.tile(inv_ref[rows, :], (1, src_ref.shape[1] // LANES))
        finish(rows, src_ref[rows, :] * inv)
        return carry

    lax.fori_loop(0, n // STAT_ROWS, stats, 0, unroll=True)
    lax.fori_loop(0, n // APPLY_ROWS, apply, 0)


def _ffn_kernel(cast_weights, x_ref, gpre_ref, gpost_ref, wup_ref, wdn_ref, o_ref, *rest):
    f = pl.program_id(1)
    h_ref, inv_ref = rest[-2:]

    @pl.when(f == 0)
    def _():
        g = gpre_ref[...]

        def store_h(rows, xn):
            h_ref[rows, :] = (xn * g).astype(BF16)
            o_ref[rows, :] = jnp.zeros((APPLY_ROWS, D_MODEL), F32)

        _rms_chunked(x_ref, inv_ref, store_h)

    if cast_weights:
        wupb_ref, wdnb_ref = rest[:2]
        wupb_ref[...] = wup_ref[...].astype(BF16)
        wdnb_ref[...] = wdn_ref[...].astype(BF16)
        wup_ref, wdn_ref = wupb_ref, wdnb_ref
    a = jnp.dot(h_ref[...], wup_ref[...], preferred_element_type=F32)
    a = jnp.square(jnp.maximum(a, 0.0)).astype(BF16)
    o_ref[...] += jnp.dot(a, wdn_ref[...], preferred_element_type=F32)

    @pl.when(f == pl.num_programs(1) - 1)
    def _():
        g = gpost_ref[...]

        def store_out(rows, yn):
            o_ref[rows, :] = x_ref[rows, :] + yn * g

        _rms_chunked(o_ref, inv_ref, store_out)


def _ffn(l, x, gpre, gpost, wup, wdn, tm, tf, cast_weights):
    m = x.shape[0]
    if cast_weights:
        assert m == tm
        w_specs = [pl.BlockSpec((None, D_MODEL, tf), lambda i, f: (l, 0, f)),
                   pl.BlockSpec((None, tf, D_MODEL), lambda i, f: (l, f, 0))]
        extra_specs = [pl.BlockSpec((D_MODEL, tf), lambda i, f: (0, f)),
                       pl.BlockSpec((tf, D_MODEL), lambda i, f: (f, 0))]
        extra_shapes = [jax.ShapeDtypeStruct((D_MODEL, D_FF), BF16), jax.ShapeDtypeStruct((D_FF, D_MODEL), BF16)]
    else:
        w_specs = [pl.BlockSpec((D_MODEL, tf), lambda i, f: (0, f)),
                   pl.BlockSpec((tf, D_MODEL), lambda i, f: (f, 0))]
        extra_specs, extra_shapes = [], []
    return pl.pallas_call(
        functools.partial(_ffn_kernel, cast_weights),
        grid=(m // tm, D_FF // tf),
        in_specs=[
            pl.BlockSpec((tm, D_MODEL), lambda i, f: (i, 0)),
            _layer_vec(l, D_MODEL),
            _layer_vec(l, D_MODEL),
        ] + w_specs,
        out_specs=[pl.BlockSpec((tm, D_MODEL), lambda i, f: (i, 0))] + extra_specs,
        out_shape=[jax.ShapeDtypeStruct((m, D_MODEL), F32)] + extra_shapes,
        scratch_shapes=[pltpu.VMEM((tm, D_MODEL), BF16), pltpu.VMEM((tm, LANES), F32)],
        compiler_params=_params("arbitrary", "arbitrary"),
        name="ffn_sample" if cast_weights else "ffn_prompt",
    )(x, gpre, gpost, wup, wdn)


def _bias_tables():
    heads = np.arange(1, N_Q_HEADS + 1, dtype=np.float32)
    slopes = np.exp2(np.float32(-8.0) * heads / np.float32(N_Q_HEADS)).astype(np.float32)

    def table(rows, mask_prev):
        i = np.arange(rows)[:, None]
        j = np.arange(2 * WINDOW)[None, :]
        dist = i + WINDOW - j
        valid = (dist >= 0) & (dist < WINDOW)
        if mask_prev:
            valid = valid & (j >= WINDOW)
        out = np.empty((2 * N_KV_HEADS, 2 * rows, 2 * WINDOW), np.float32)
        for kh in range(N_KV_HEADS):
            for par in range(2):
                for half, head in enumerate((4 * kh + par, 4 * kh + 2 + par)):
                    bias = (-slopes[head]) * dist.astype(np.float32)
                    out[2 * kh + par, half * rows:(half + 1) * rows] = np.where(valid, bias, np.float32(NEG_BIG))
        return out

    prompt = np.stack([table(WINDOW, False), table(WINDOW, True)])
    sample = table(SAMPLE_ROWS, False)
    return jnp.asarray(prompt), jnp.asarray(sample)


def kernel(x_prompt, x_sample, cache_k, cache_v, state_pool, w_in, w_out, w_pool, pool_scale, attn_sinks,
           g_pre_mix, g_post_mix, g_pre_ffn, g_post_ffn, w_up, w_down):
    n_seq, seq, _ = x_prompt.shape
    dec_batch, dec_seq, _ = x_sample.shape
    assert seq % MIX_TM == 0 and (n_seq * seq) % FFN_TM == 0 and dec_seq <= SAMPLE_ROWS
    bias_prompt, bias_sample = _bias_tables()

    vec = lambda p: p.reshape(DEPTH, 1, p.shape[-1])
    gpm, gqm, gpf, gqf, pscale = vec(g_pre_mix), vec(g_post_mix), vec(g_pre_ffn), vec(g_post_ffn), vec(pool_scale)
    w_pool_b = w_pool.astype(BF16)

    xp = x_prompt.reshape(n_seq * seq, D_MODEL)
    xs = jnp.pad(x_sample, ((0, 0), (0, SAMPLE_ROWS - dec_seq), (0, 0))).reshape(dec_batch * SAMPLE_ROWS, D_MODEL)
    ck_all = cache_k.reshape(DEPTH, dec_batch, WINDOW, KV_WIDTH)
    cv_all = cache_v.reshape(DEPTH, dec_batch, WINDOW, KV_WIDTH)
    st_all = jnp.pad(state_pool, ((0, 0), (0, 0), (1, 0), (0, 0)))

    kp = jnp.zeros((DEPTH, n_seq, WINDOW, KV_WIDTH), F32)
    vp = jnp.zeros((DEPTH, n_seq, WINDOW, KV_WIDTH), F32)
    pp = jnp.zeros((DEPTH, n_seq, POOL_STATE, POOL_WIDTH), F32)
    ks = jnp.zeros((DEPTH, dec_batch, WINDOW, KV_WIDTH), F32)
    vs = jnp.zeros((DEPTH, dec_batch, WINDOW, KV_WIDTH), F32)
    ps = jnp.zeros((DEPTH, dec_batch, POOL_STATE, POOL_WIDTH), F32)

    for l in range(DEPTH):
        proj, w_in_b = _inproj_sample(l, xs, gpm, w_in)
        mix, ks, vs, ps = _mixer_sample(l, proj, ck_all, cv_all, st_all, attn_sinks, bias_sample, w_pool_b, pscale,
                                        ks, vs, ps, dec_seq)
        xs, w_out_b = _outproj_sample(l, mix.reshape(xs.shape[0], MIX_WIDTH), xs, gqm, w_out)
        xs, w_up_b, w_down_b = _ffn(l, xs, gpf, gqf, w_up, w_down, xs.shape[0], WEIGHT_BLOCK, True)

        xp, kp, vp, pp = _mixer_prompt(l, xp, attn_sinks, gpm, gqm, w_in_b, w_out_b, bias_prompt, w_pool_b, pscale,
                                       kp, vp, pp, seq)
        (xp,) = _ffn(l, xp, gpf, gqf, w_up_b, w_down_b, FFN_TM, FFN_TF, False)

    y_prompt = xp.reshape(n_seq, seq, D_MODEL)
    y_sample = xs.reshape(dec_batch, SAMPLE_ROWS, D_MODEL)[:, :dec_seq]
    kv_shape = (WINDOW, N_KV_HEADS, HEAD_DIM)
    return (y_prompt, y_sample, kp.reshape((DEPTH, n_seq) + kv_shape), vp.reshape((DEPTH, n_seq) + kv_shape), pp,
            ks.reshape((DEPTH, dec_batch) + kv_shape), vs.reshape((DEPTH, dec_batch) + kv_shape), ps)
```

```python
import functools

import jax
import jax.numpy as jnp
import numpy as np
from jax import lax
from jax.experimental import pallas as pl
from jax.experimental.pallas import tpu as pltpu

D_MODEL = 2048
DEPTH = 4
PAST_LEN = 16384
HEAD_DIM = 64
N_Q_HEADS = 16
N_KV_HEADS = 4
ATTN_WIDTH = N_Q_HEADS * HEAD_DIM
KV_WIDTH = N_KV_HEADS * HEAD_DIM
WINDOW = 128
POOL_WINDOWS = (2, 4, 8, 16)
POOL_GROUP_WIDTH = 256
POOL_WIDTH = len(POOL_WINDOWS) * POOL_GROUP_WIDTH
POOL_STATE = max(POOL_WINDOWS) - 1
POOL_PREV_ROWS = POOL_STATE + 1
MIX_WIDTH = ATTN_WIDTH + POOL_WIDTH
IN_WIDTH = ATTN_WIDTH + 2 * KV_WIDTH + POOL_WIDTH
D_FF = 4 * D_MODEL
EPS = 1e-6
Q_SCALE = HEAD_DIM ** -0.5

LANES = 128
SUBLANES = 8
NEG_BIG = -1e30
VMEM_LIMIT_BYTES = 56 * 1024 * 1024
SAMPLE_ROWS = SUBLANES

FFN_TM = 512
FFN_TF = 1024
STAT_ROWS = 256
APPLY_ROWS = 64
SAMPLE_GROUP = 4
WEIGHT_BLOCK = 512
MIX_TM = 256
MXU_CHUNK = 256
MIX_LAG = 2
KV_PADS = 4

BF16 = jnp.bfloat16
F32 = jnp.float32


def _rms(x, g):
    ms = jnp.mean(x * x, axis=-1, keepdims=True)
    return (x * lax.rsqrt(ms + EPS)) * g


def _params(*semantics):
    return pltpu.CompilerParams(dimension_semantics=semantics, vmem_limit_bytes=VMEM_LIMIT_BYTES)


def _layer_vec(l, width):
    return pl.BlockSpec((None, 1, width), lambda *_: (l, 0, 0))


def _resident(shape, index_map):
    return pl.BlockSpec(shape, index_map, pipeline_mode=pl.Buffered(1))


def _last_rows(x, n):
    tail = x[x.shape[0] - 2 * SUBLANES:]
    return pltpu.roll(tail, n, axis=0)[:n]


def _half_padded(x2, head_in_high_half):
    lane = lax.broadcasted_iota(jnp.int32, x2.shape, 1)
    swapped = pltpu.roll(x2, HEAD_DIM, axis=1)
    in_lo, in_hi = (swapped, x2) if head_in_high_half else (x2, swapped)
    lo = jnp.where(lane < HEAD_DIM, in_lo, 0.0).astype(BF16)
    hi = jnp.where(lane >= HEAD_DIM, in_hi, 0.0).astype(BF16)
    return lo, hi


def _kv_pads(k, v, kh):
    col = (kh // 2) * LANES
    high = kh % 2 == 1
    return _half_padded(k[:, col:col + LANES], high) + _half_padded(v[:, col:col + LANES], high)


def _attention(items, rows, sink_at, between=lambda: None):
    keys = 2 * WINDOW
    row = lax.broadcasted_iota(jnp.int32, (2 * rows, 1), 0)
    lane = lax.broadcasted_iota(jnp.int32, (2 * rows, LANES), 1)

    scores = []
    for kh, q_at, (k_lo, k_hi, _, _), bias_at, _ in items:
        c0 = 2 * kh * LANES
        qq = jnp.concatenate([q_at(c0), q_at(c0 + LANES)], axis=0).astype(BF16)
        s = lax.dot_general(qq, jnp.concatenate([k_lo, k_hi], axis=0), (((1,), (1,)), ((), ())),
                            preferred_element_type=F32)
        scores.append([s[:, par * keys:(par + 1) * keys] + bias_at(2 * kh + par) for par in range(2)])
        between()

    sinks = [[jnp.where(row < rows, sink_at(4 * kh + par), sink_at(4 * kh + 2 + par)) for par in range(2)]
             for kh, *_ in items]
    maxes = []
    for pair, sks in zip(scores, sinks):
        maxes.append([jnp.maximum(jnp.max(s, axis=-1, keepdims=True), sk) for s, sk in zip(pair, sks)])
        between()
    probs, recips = [], []
    for pair, sks, ms in zip(scores, sinks, maxes):
        probs.append([jnp.exp(s - m) for s, m in zip(pair, ms)])
        recips.append([1.0 / (jnp.sum(p, axis=-1, keepdims=True) + jnp.exp(sk - m))
                       for p, sk, m in zip(probs[-1], sks, ms)])
        between()

    for (kh, _, (_, _, v_lo, v_hi), _, out_ref), pair, rcp in zip(items, probs, recips):
        c0 = 2 * kh * LANES
        p_cat = jnp.concatenate([p.astype(BF16) for p in pair], axis=1)
        o = jnp.dot(p_cat, jnp.concatenate([v_lo, v_hi], axis=0), preferred_element_type=F32)
        o = o * jnp.where(lane < HEAD_DIM, rcp[0], rcp[1])
        out_ref[:, c0:c0 + LANES] = o[:rows].astype(out_ref.dtype)
        out_ref[:, c0 + LANES:c0 + 2 * LANES] = o[rows:].astype(out_ref.dtype)
        between()


def _pool(rows, u_prev, u_cur, pos, wpool_ref, pscale_ref, out_ref):
    ext = jnp.concatenate([u_prev, u_cur], axis=0)
    for gi, w in enumerate(POOL_WINDOWS):
        lo = gi * POOL_GROUP_WIDTH
        e = ext[:, lo:lo + POOL_GROUP_WIDTH]
        s, d = e, 1
        while d < w:
            s = s + pltpu.roll(s, d, axis=0)
            d *= 2
        cnt = jnp.minimum(pos + 1, w).astype(F32)
        z = s[POOL_PREV_ROWS:] / cnt - e[POOL_PREV_ROWS:]
        zz = jnp.dot(z.astype(BF16), wpool_ref[gi], preferred_element_type=F32)
        zz = zz * pscale_ref[:, lo:lo + POOL_GROUP_WIDTH]
        out_ref[:, ATTN_WIDTH + lo:ATTN_WIDTH + lo + POOL_GROUP_WIDTH] = zz.astype(out_ref.dtype)


def _mixer_prompt_kernel(l, tiles_per_seq, n_tiles, sinks_ref, x_ref, xo_ref, gpre_ref, gpost_ref, gffn_ref, win_ref,
                         wout_ref, bias_ref, wpool_ref, pscale_ref, kp_any, vp_any, pp_any,
                         out_ref, hf_ref, kp_ref, vp_ref, pp_ref, *scratch):
    del kp_any, vp_any, pp_any
    q_s, kv_s, u_s, mix_s = scratch[0:2], scratch[2:4], scratch[4:6], scratch[6:8]
    y_s, pads_s, utail_s = scratch[8:]
    s = pl.program_id(0)
    blocks = MIX_TM // WINDOW
    block_rows = [pl.ds(j * WINDOW, WINDOW) for j in range(blocks)]

    @pl.when(s == 0)
    def _():
        pads_s[...] = jnp.zeros_like(pads_s)
        utail_s[...] = jnp.zeros_like(utail_s)

    def step(new, project_in, attend, project_out):
        old = 1 - new
        if project_in:
            h = _rms(x_ref[...], gpre_ref[...]).astype(BF16)

        def in_projection_chunk(c):
            r = jnp.dot(h, win_ref[:, c:c + MXU_CHUNK], preferred_element_type=F32)
            if c < ATTN_WIDTH:
                q_s[new][:, c:c + MXU_CHUNK] = r * Q_SCALE
            elif c < ATTN_WIDTH + 2 * KV_WIDTH:
                kv_s[new][:, c - ATTN_WIDTH:c - ATTN_WIDTH + MXU_CHUNK] = r
            else:
                c -= ATTN_WIDTH + 2 * KV_WIDTH
                u_s[new][:, c:c + MXU_CHUNK] = r

        def out_projection_chunk(c):
            y_s[:, c:c + MXU_CHUNK] = jnp.dot(mix_s[new][...], wout_ref[:, c:c + MXU_CHUNK],
                                              preferred_element_type=F32)

        def out_projection_finish():
            x_new = xo_ref[...] + _rms(y_s[...], gpost_ref[...])
            out_ref[...] = x_new
            hf_ref[...] = _rms(x_new, gffn_ref[...]).astype(BF16)

        pending = []
        if project_out:
            pending += [functools.partial(out_projection_chunk, c) for c in range(0, D_MODEL, MXU_CHUNK)]
            pending += [out_projection_finish]
        if project_in:
            pending += [functools.partial(in_projection_chunk, c) for c in range(0, IN_WIDTH, MXU_CHUNK)]
        n_pending = len(pending)
        n_slots = 4 * N_KV_HEADS * blocks
        slots = [0]

        def between():
            slots[0] += 1
            while n_pending - len(pending) < slots[0] * n_pending // n_slots:
                pending.pop(0)()

        if not attend:
            while pending:
                pending.pop(0)()
            if project_in:
                state_outputs(new)
            return

        tile = (s - 1) % tiles_per_seq
        seq_start = tile == 0
        first = seq_start.astype(jnp.int32)
        prev_pads = [pads_s[i] for i in range(KV_PADS * N_KV_HEADS)]
        items = []
        for kh in range(N_KV_HEADS):
            pads = prev_pads[KV_PADS * kh:KV_PADS * (kh + 1)]
            for j, rows in enumerate(block_rows):
                cur = _kv_pads(kv_s[old][rows, :KV_WIDTH], kv_s[old][rows, KV_WIDTH:], kh)
                both = [jnp.concatenate([p, c], axis=0) for p, c in zip(pads, cur)]
                bias_at = (lambda i: bias_ref[first, i]) if j == 0 else (lambda i: bias_ref[0, i])
                items.append((kh, functools.partial(lambda rows, c: q_s[old][rows, c:c + LANES], rows), both,
                              bias_at, mix_s[old].at[rows]))
                pads = cur
            for i, p in enumerate(pads):
                pads_s[KV_PADS * kh + i] = p
        _attention(items, WINDOW, lambda hd: sinks_ref[l, hd], between)
        assert not pending

        u_prev = jnp.where(seq_start, 0.0, utail_s[...])
        for j, rows in enumerate(block_rows):
            u_cur = u_s[old][rows, :]
            pos = (tile * blocks + j) * WINDOW + lax.broadcasted_iota(jnp.int32, (WINDOW, 1), 0)
            _pool(WINDOW, u_prev, u_cur, pos, wpool_ref, pscale_ref, mix_s[old].at[rows])
            u_prev = u_cur[WINDOW - POOL_PREV_ROWS:]
        utail_s[...] = u_prev
        if project_in:
            state_outputs(new)

    def state_outputs(new):
        @pl.when(s % tiles_per_seq == tiles_per_seq - 1)
        def _():
            kp_ref[...] = kv_s[new][MIX_TM - WINDOW:, :KV_WIDTH]
            vp_ref[...] = kv_s[new][MIX_TM - WINDOW:, KV_WIDTH:]
            pp_ref[...] = _last_rows(u_s[new][MIX_TM - 2 * SUBLANES:, :], POOL_STATE)

    pl.when(s == 0)(functools.partial(step, 0, True, False, False))
    pl.when(s == 1)(functools.partial(step, 1, True, True, False))
    for new in range(2):
        pl.when((s >= MIX_LAG) & (s < n_tiles) & (s % 2 == new))(functools.partial(step, new, True, True, True))
    pl.when(s == n_tiles)(functools.partial(step, 0, False, True, True))
    pl.when(s == n_tiles + 1)(functools.partial(step, 1, False, False, True))


def _mixer_prompt(l, x, sinks, gpre, gpost, gffn, w_in_b, w_out_b, bias, wpool, pscale, kp, vp, pp, seq):
    m = x.shape[0]
    tm = MIX_TM
    tiles_per_seq = seq // tm
    n_tiles = m // tm
    assert n_tiles % 2 == 0 and n_tiles > MIX_LAG
    cur = lambda s: (jnp.minimum(s, n_tiles - 1), 0)
    lagging = lambda s: (jnp.maximum(s - MIX_LAG, 0), 0)
    state = lambda rows, width: pl.BlockSpec((None, None, rows, width),
                                             lambda s: (l, jnp.minimum(s, n_tiles - 1) // tiles_per_seq, 0, 0))
    return pl.pallas_call(
        functools.partial(_mixer_prompt_kernel, l, tiles_per_seq, n_tiles),
        grid=(n_tiles + MIX_LAG,),
        in_specs=[
            pl.BlockSpec(memory_space=pltpu.SMEM),
            pl.BlockSpec((tm, D_MODEL), cur),
            pl.BlockSpec((tm, D_MODEL), lagging),
            _layer_vec(l, D_MODEL),
            _layer_vec(l, D_MODEL),
            _layer_vec(l, D_MODEL),
            _resident((D_MODEL, IN_WIDTH), lambda s: (0, 0)),
            _resident((MIX_WIDTH, D_MODEL), lambda s: (0, 0)),
            _resident(bias.shape, lambda s: (0, 0, 0, 0)),
            _resident((None,) + wpool.shape[1:], lambda s: (l, 0, 0, 0)),
            _layer_vec(l, POOL_WIDTH),
            pl.BlockSpec(memory_space=pl.ANY),
            pl.BlockSpec(memory_space=pl.ANY),
            pl.BlockSpec(memory_space=pl.ANY),
        ],
        out_specs=[
            pl.BlockSpec((tm, D_MODEL), lagging),
            pl.BlockSpec((tm, D_MODEL), lagging),
            state(WINDOW, KV_WIDTH),
            state(WINDOW, KV_WIDTH),
            state(POOL_STATE, POOL_WIDTH),
        ],
        out_shape=[
            jax.ShapeDtypeStruct((m, D_MODEL), F32),
            jax.ShapeDtypeStruct((m, D_MODEL), BF16),
            jax.ShapeDtypeStruct(kp.shape, F32),
            jax.ShapeDtypeStruct(vp.shape, F32),
            jax.ShapeDtypeStruct(pp.shape, F32),
        ],
        scratch_shapes=[
            pltpu.VMEM((tm, ATTN_WIDTH), F32), pltpu.VMEM((tm, ATTN_WIDTH), F32),
            pltpu.VMEM((tm, 2 * KV_WIDTH), F32), pltpu.VMEM((tm, 2 * KV_WIDTH), F32),
            pltpu.VMEM((tm, POOL_WIDTH), F32), pltpu.VMEM((tm, POOL_WIDTH), F32),
            pltpu.VMEM((tm, MIX_WIDTH), BF16), pltpu.VMEM((tm, MIX_WIDTH), BF16),
            pltpu.VMEM((tm, D_MODEL), F32),
            pltpu.VMEM((KV_PADS * N_KV_HEADS, WINDOW, LANES), BF16),
            pltpu.VMEM((POOL_PREV_ROWS, POOL_WIDTH), F32),
        ],
        input_output_aliases={11: 2, 12: 3, 13: 4},
        compiler_params=_params("arbitrary"),
        name="mixer_prompt",
    )(sinks, x, x, gpre, gpost, gffn, w_in_b, w_out_b, bias, wpool, pscale, kp, vp, pp)


def _inproj_sample_kernel(x_ref, g_ref, w_ref, proj_ref, wb_ref, h_ref):
    j = pl.program_id(0)

    @pl.when(j == 0)
    def _():
        h_ref[...] = _rms(x_ref[...], g_ref[...]).astype(BF16)

    wb = w_ref[...].astype(BF16)
    wb_ref[...] = wb
    p = jnp.dot(h_ref[...], wb, preferred_element_type=F32)
    proj_ref[...] = p * jnp.where(j < ATTN_WIDTH // WEIGHT_BLOCK, Q_SCALE, 1.0)


def _inproj_sample(l, x, g, w):
    m = x.shape[0]
    tn = WEIGHT_BLOCK
    return pl.pallas_call(
        _inproj_sample_kernel,
        grid=(IN_WIDTH // tn,),
        in_specs=[
            pl.BlockSpec((m, D_MODEL), lambda j: (0, 0)),
            _layer_vec(l, D_MODEL),
            pl.BlockSpec((None, D_MODEL, tn), lambda j: (l, 0, j)),
        ],
        out_specs=[
            pl.BlockSpec((m, tn), lambda j: (0, j)),
            pl.BlockSpec((D_MODEL, tn), lambda j: (0, j)),
        ],
        out_shape=[
            jax.ShapeDtypeStruct((m, IN_WIDTH), F32),
            jax.ShapeDtypeStruct((D_MODEL, IN_WIDTH), BF16),
        ],
        scratch_shapes=[pltpu.VMEM((m, D_MODEL), BF16)],
        compiler_params=_params("arbitrary"),
        name="inproj_sample",
    )(x, g, w)


def _shift_in(old, new, n_new):
    r = old.shape[0]
    rolled = pltpu.roll(old, r - n_new, axis=0)
    tail = pltpu.roll(new, SUBLANES - n_new, axis=0)
    row = lax.broadcasted_iota(jnp.int32, (SUBLANES, old.shape[1]), 0)
    last = jnp.where(row < SUBLANES - n_new, rolled[r - SUBLANES:], tail)
    return jnp.concatenate([rolled[:r - SUBLANES], last], axis=0)


def _mixer_sample_kernel(l, n_new, sinks_ref, proj_ref, ck_ref, cv_ref, st_ref, bias_ref, wpool_ref, pscale_ref,
                         ks_any, vs_any, ps_any, out_ref, ko_ref, vo_ref, po_ref):
    del ks_any, vs_any, ps_any
    rows = SAMPLE_ROWS
    pad = jnp.zeros((WINDOW - rows, KV_WIDTH), F32)
    pos = PAST_LEN + lax.broadcasted_iota(jnp.int32, (rows, 1), 0)
    seqs = range(SAMPLE_GROUP)
    k_new = [proj_ref[g, :, ATTN_WIDTH:ATTN_WIDTH + KV_WIDTH] for g in seqs]
    v_new = [proj_ref[g, :, ATTN_WIDTH + KV_WIDTH:ATTN_WIDTH + 2 * KV_WIDTH] for g in seqs]
    kk = [jnp.concatenate([ck_ref[g], k_new[g], pad], axis=0) for g in seqs]
    vv = [jnp.concatenate([cv_ref[g], v_new[g], pad], axis=0) for g in seqs]
    items = [(kh, functools.partial(lambda g, c: proj_ref[g, :, c:c + LANES], g), _kv_pads(kk[g], vv[g], kh),
              lambda i: bias_ref[i], out_ref.at[g]) for kh in range(N_KV_HEADS) for g in seqs]
    _attention(items, rows, lambda hd: sinks_ref[l, hd])
    for g in seqs:
        u_prev, u_cur = st_ref[g], proj_ref[g, :, ATTN_WIDTH + 2 * KV_WIDTH:]
        _pool(rows, u_prev, u_cur, pos, wpool_ref, pscale_ref, out_ref.at[g])
        ko_ref[g] = _shift_in(ck_ref[g], k_new[g], n_new)
        vo_ref[g] = _shift_in(cv_ref[g], v_new[g], n_new)
        po_ref[g] = _last_rows(_shift_in(u_prev, u_cur, n_new), POOL_STATE)


def _mixer_sample(l, proj, cache_k, cache_v, state, sinks, bias, wpool, pscale, ks, vs, ps, n_new):
    n_seq = cache_k.shape[1]
    rows = SAMPLE_ROWS
    grp = SAMPLE_GROUP
    assert n_seq % grp == 0
    per_seq = lambda r, width: pl.BlockSpec((None, grp, r, width), lambda n: (l, n, 0, 0))
    seqs = lambda width: pl.BlockSpec((grp, rows, width), lambda n: (n, 0, 0))
    return pl.pallas_call(
        functools.partial(_mixer_sample_kernel, l, n_new),
        grid=(n_seq // grp,),
        in_specs=[
            pl.BlockSpec(memory_space=pltpu.SMEM),
            seqs(IN_WIDTH),
            per_seq(WINDOW, KV_WIDTH),
            per_seq(WINDOW, KV_WIDTH),
            per_seq(POOL_PREV_ROWS, POOL_WIDTH),
            pl.BlockSpec(bias.shape, lambda n: (0, 0, 0)),
            pl.BlockSpec((None,) + wpool.shape[1:], lambda n: (l, 0, 0, 0)),
            _layer_vec(l, POOL_WIDTH),
            pl.BlockSpec(memory_space=pl.ANY),
            pl.BlockSpec(memory_space=pl.ANY),
            pl.BlockSpec(memory_space=pl.ANY),
        ],
        out_specs=[
            seqs(MIX_WIDTH),
            per_seq(WINDOW, KV_WIDTH),
            per_seq(WINDOW, KV_WIDTH),
            per_seq(POOL_STATE, POOL_WIDTH),
        ],
        out_shape=[
            jax.ShapeDtypeStruct((n_seq, rows, MIX_WIDTH), F32),
            jax.ShapeDtypeStruct(ks.shape, F32),
            jax.ShapeDtypeStruct(vs.shape, F32),
            jax.ShapeDtypeStruct(ps.shape, F32),
        ],
        input_output_aliases={8: 1, 9: 2, 10: 3},
        compiler_params=_params("arbitrary"),
        name="mixer_sample",
    )(sinks, proj.reshape(n_seq, rows, IN_WIDTH), cache_k, cache_v, state, bias, wpool, pscale, ks, vs, ps)


def _outproj_sample_kernel(mix_ref, x_ref, g_ref, gffn_ref, w_ref, o_ref, hf_ref, wb_ref):
    k = pl.program_id(0)

    @pl.when(k == 0)
    def _():
        o_ref[...] = jnp.zeros_like(o_ref)

    wb = w_ref[...].astype(BF16)
    wb_ref[...] = wb
    o_ref[...] += jnp.dot(mix_ref[...].astype(BF16), wb, preferred_element_type=F32)

    @pl.when(k == pl.num_programs(0) - 1)
    def _():
        x_new = x_ref[...] + _rms(o_ref[...], g_ref[...])
        o_ref[...] = x_new
        hf_ref[...] = _rms(x_new, gffn_ref[...]).astype(BF16)


def _outproj_sample(l, mix, x, g, gffn, w):
    m = x.shape[0]
    tk = WEIGHT_BLOCK
    return pl.pallas_call(
        _outproj_sample_kernel,
        grid=(MIX_WIDTH // tk,),
        in_specs=[
            pl.BlockSpec((m, tk), lambda k: (0, k)),
            pl.BlockSpec((m, D_MODEL), lambda k: (0, 0)),
            _layer_vec(l, D_MODEL),
            _layer_vec(l, D_MODEL),
            pl.BlockSpec((None, tk, D_MODEL), lambda k: (l, k, 0)),
        ],
        out_specs=[
            pl.BlockSpec((m, D_MODEL), lambda k: (0, 0)),
            pl.BlockSpec((m, D_MODEL), lambda k: (0, 0)),
            pl.BlockSpec((tk, D_MODEL), lambda k: (k, 0)),
        ],
        out_shape=[
            jax.ShapeDtypeStruct((m, D_MODEL), F32),
            jax.ShapeDtypeStruct((m, D_MODEL), BF16),
            jax.ShapeDtypeStruct((MIX_WIDTH, D_MODEL), BF16),
        ],
        compiler_params=_params("arbitrary"),
        name="outproj_sample",
    )(mix, x, g, gffn, w)


def _rms_chunked(src_ref, inv_ref, finish):
    n = src_ref.shape[0]

    def stats(c, carry):
        rows = pl.ds(pl.multiple_of(c * STAT_ROWS, STAT_ROWS), STAT_ROWS)
        y = src_ref[rows, :]
        inv = lax.rsqrt(jnp.mean(y * y, axis=-1, keepdims=True) + EPS)
        inv_ref[rows, :] = jnp.broadcast_to(inv, (STAT_ROWS, LANES))
        return carry

    def apply(c, carry):
        rows = pl.ds(pl.multiple_of(c * APPLY_ROWS, APPLY_ROWS), APPLY_ROWS)
        inv = jnp.tile(inv_ref[rows, :], (1, src_ref.shape[1] // LANES))
        finish(rows, src_ref[rows, :] * inv)
        return carry

    lax.fori_loop(0, n // STAT_ROWS, stats, 0, unroll=True)
    lax.fori_loop(0, n // APPLY_ROWS, apply, 0)


def _ffn_kernel(cast_weights, h_ref, x_ref, gpost_ref, wup_ref, wdn_ref, o_ref, *rest):
    f = pl.program_id(1)
    inv_ref = rest[-1]

    if cast_weights:
        wupb_ref, wdnb_ref = rest[:2]
        wupb_ref[...] = wup_ref[...].astype(BF16)
        wdnb_ref[...] = wdn_ref[...].astype(BF16)
        wup_ref, wdn_ref = wupb_ref, wdnb_ref

    def block(first):
        a = jnp.dot(h_ref[...], wup_ref[...], preferred_element_type=F32)
        a = jnp.square(jnp.maximum(a, 0.0)).astype(BF16)
        d = jnp.dot(a, wdn_ref[...], preferred_element_type=F32)
        if first:
            o_ref[...] = d
        else:
            o_ref[...] += d

    pl.when(f == 0)(functools.partial(block, True))
    pl.when(f > 0)(functools.partial(block, False))

    @pl.when(f == pl.num_programs(1) - 1)
    def _():
        g = gpost_ref[...]

        def store_out(rows, yn):
            o_ref[rows, :] = x_ref[rows, :] + yn * g

        _rms_chunked(o_ref, inv_ref, store_out)


def _ffn(l, h, x, gpost, wup, wdn, tm, tf, cast_weights):
    m = x.shape[0]
    if cast_weights:
        assert m == tm
        w_specs = [pl.BlockSpec((None, D_MODEL, tf), lambda i, f: (l, 0, f)),
                   pl.BlockSpec((None, tf, D_MODEL), lambda i, f: (l, f, 0))]
        extra_specs = [pl.BlockSpec((D_MODEL, tf), lambda i, f: (0, f)),
                       pl.BlockSpec((tf, D_MODEL), lambda i, f: (f, 0))]
        extra_shapes = [jax.ShapeDtypeStruct((D_MODEL, D_FF), BF16), jax.ShapeDtypeStruct((D_FF, D_MODEL), BF16)]
    else:
        w_specs = [pl.BlockSpec((D_MODEL, tf), lambda i, f: (0, f)),
                   pl.BlockSpec((tf, D_MODEL), lambda i, f: (f, 0))]
        extra_specs, extra_shapes = [], []
    return pl.pallas_call(
        functools.partial(_ffn_kernel, cast_weights),
        grid=(m // tm, D_FF // tf),
        in_specs=[
            pl.BlockSpec((tm, D_MODEL), lambda i, f: (i, 0)),
            pl.BlockSpec((tm, D_MODEL), lambda i, f: (i, 0)),
            _layer_vec(l, D_MODEL),
        ] + w_specs,
        out_specs=[pl.BlockSpec((tm, D_MODEL), lambda i, f: (i, 0))] + extra_specs,
        out_shape=[jax.ShapeDtypeStruct((m, D_MODEL), F32)] + extra_shapes,
        scratch_shapes=[pltpu.VMEM((tm, LANES), F32)],
        compiler_params=_params("arbitrary", "arbitrary"),
        name="ffn_sample" if cast_weights else "ffn_prompt",
    )(h, x, gpost, wup, wdn)


def _bias_tables():
    heads = np.arange(1, N_Q_HEADS + 1, dtype=np.float32)
    slopes = np.exp2(np.float32(-8.0) * heads / np.float32(N_Q_HEADS)).astype(np.float32)

    def table(rows, mask_prev):
        i = np.arange(rows)[:, None]
        j = np.arange(2 * WINDOW)[None, :]
        dist = i + WINDOW - j
        valid = (dist >= 0) & (dist < WINDOW)
        if mask_prev:
            valid = valid & (j >= WINDOW)
        out = np.empty((2 * N_KV_HEADS, 2 * rows, 2 * WINDOW), np.float32)
        for kh in range(N_KV_HEADS):
            for par in range(2):
                for half, head in enumerate((4 * kh + par, 4 * kh + 2 + par)):
                    bias = (-slopes[head]) * dist.astype(np.float32)
                    out[2 * kh + par, half * rows:(half + 1) * rows] = np.where(valid, bias, np.float32(NEG_BIG))
        return out

    prompt = np.stack([table(WINDOW, False), table(WINDOW, True)])
    sample = table(SAMPLE_ROWS, False)
    return jnp.asarray(prompt), jnp.asarray(sample)


def kernel(x_prompt, x_sample, cache_k, cache_v, state_pool, w_in, w_out, w_pool, pool_scale, attn_sinks,
           g_pre_mix, g_post_mix, g_pre_ffn, g_post_ffn, w_up, w_down):
    n_seq, seq, _ = x_prompt.shape
    dec_batch, dec_seq, _ = x_sample.shape
    assert seq % MIX_TM == 0 and (n_seq * seq) % FFN_TM == 0 and dec_seq <= SAMPLE_ROWS
    bias_prompt, bias_sample = _bias_tables()

    vec = lambda p: p.reshape(DEPTH, 1, p.shape[-1])
    gpm, gqm, gpf, gqf, pscale = vec(g_pre_mix), vec(g_post_mix), vec(g_pre_ffn), vec(g_post_ffn), vec(pool_scale)
    w_pool_b = w_pool.astype(BF16)

    xp = x_prompt.reshape(n_seq * seq, D_MODEL)
    xs = jnp.pad(x_sample, ((0, 0), (0, SAMPLE_ROWS - dec_seq), (0, 0))).reshape(dec_batch * SAMPLE_ROWS, D_MODEL)
    ck_all = cache_k.reshape(DEPTH, dec_batch, WINDOW, KV_WIDTH)
    cv_all = cache_v.reshape(DEPTH, dec_batch, WINDOW, KV_WIDTH)
    st_all = jnp.pad(state_pool, ((0, 0), (0, 0), (1, 0), (0, 0)))

    kp = jnp.zeros((DEPTH, n_seq, WINDOW, KV_WIDTH), F32)
    vp = jnp.zeros((DEPTH, n_seq, WINDOW, KV_WIDTH), F32)
    pp = jnp.zeros((DEPTH, n_seq, POOL_STATE, POOL_WIDTH), F32)
    ks = jnp.zeros((DEPTH, dec_batch, WINDOW, KV_WIDTH), F32)
    vs = jnp.zeros((DEPTH, dec_batch, WINDOW, KV_WIDTH), F32)
    ps = jnp.zeros((DEPTH, dec_batch, POOL_STATE, POOL_WIDTH), F32)

    for l in range(DEPTH):
        proj, w_in_b = _inproj_sample(l, xs, gpm, w_in)
        mix, ks, vs, ps = _mixer_sample(l, proj, ck_all, cv_all, st_all, attn_sinks, bias_sample, w_pool_b, pscale,
                                        ks, vs, ps, dec_seq)
        xs, hs, w_out_b = _outproj_sample(l, mix.reshape(xs.shape[0], MIX_WIDTH), xs, gqm, gpf, w_out)
        xs, w_up_b, w_down_b = _ffn(l, hs, xs, gqf, w_up, w_down, xs.shape[0], WEIGHT_BLOCK, True)

        xp, hp, kp, vp, pp = _mixer_prompt(l, xp, attn_sinks, gpm, gqm, gpf, w_in_b, w_out_b, bias_prompt, w_pool_b,
                                           pscale, kp, vp, pp, seq)
        (xp,) = _ffn(l, hp, xp, gqf, w_up_b, w_down_b, FFN_TM, FFN_TF, False)

    y_prompt = xp.reshape(n_seq, seq, D_MODEL)
    y_sample = xs.reshape(dec_batch, SAMPLE_ROWS, D_MODEL)[:, :dec_seq]
    kv_shape = (WINDOW, N_KV_HEADS, HEAD_DIM)
    return (y_prompt, y_sample, kp.reshape((DEPTH, n_seq) + kv_shape), vp.reshape((DEPTH, n_seq) + kv_shape), pp,
            ks.reshape((DEPTH, dec_batch) + kv_shape), vs.reshape((DEPTH, dec_batch) + kv_shape), ps)
```

```python
import functools

import jax
import jax.numpy as jnp
import numpy as np
from jax import lax
from jax.experimental import pallas as pl
from jax.experimental.pallas import tpu as pltpu

D_MODEL = 2048
DEPTH = 4
PAST_LEN = 16384
HEAD_DIM = 64
N_Q_HEADS = 16
N_KV_HEADS = 4
ATTN_WIDTH = N_Q_HEADS * HEAD_DIM
KV_WIDTH = N_KV_HEADS * HEAD_DIM
WINDOW = 128
POOL_WINDOWS = (2, 4, 8, 16)
POOL_GROUP_WIDTH = 256
POOL_WIDTH = len(POOL_WINDOWS) * POOL_GROUP_WIDTH
POOL_STATE = max(POOL_WINDOWS) - 1
POOL_PREV_ROWS = POOL_STATE + 1
MIX_WIDTH = ATTN_WIDTH + POOL_WIDTH
IN_WIDTH = ATTN_WIDTH + 2 * KV_WIDTH + POOL_WIDTH
D_FF = 4 * D_MODEL
EPS = 1e-6
Q_SCALE = HEAD_DIM ** -0.5

LANES = 128
SUBLANES = 8
NEG_BIG = -1e30
VMEM_LIMIT_BYTES = 56 * 1024 * 1024
SAMPLE_ROWS = SUBLANES

FFN_TM = 512
FFN_TF = 1024
FIRST_TF = 512
STAT_ROWS = 256
APPLY_ROWS = 64
SAMPLE_GROUP = 4
WEIGHT_BLOCK = 512
MIX_TM = 256
MXU_CHUNK = 256
MIX_LAG = 2
KV_PADS = 4

BF16 = jnp.bfloat16
F32 = jnp.float32


def _rms(x, g):
    ms = jnp.mean(x * x, axis=-1, keepdims=True)
    return (x * lax.rsqrt(ms + EPS)) * g


def _params(*semantics):
    return pltpu.CompilerParams(dimension_semantics=semantics, vmem_limit_bytes=VMEM_LIMIT_BYTES)


def _layer_vec(l, width):
    return pl.BlockSpec((None, 1, width), lambda *_: (l, 0, 0))


def _resident(shape, index_map):
    return pl.BlockSpec(shape, index_map, pipeline_mode=pl.Buffered(1))


def _last_rows(x, n):
    tail = x[x.shape[0] - 2 * SUBLANES:]
    return pltpu.roll(tail, n, axis=0)[:n]


def _half_padded(x2, head_in_high_half):
    lane = lax.broadcasted_iota(jnp.int32, x2.shape, 1)
    swapped = pltpu.roll(x2, HEAD_DIM, axis=1)
    in_lo, in_hi = (swapped, x2) if head_in_high_half else (x2, swapped)
    lo = jnp.where(lane < HEAD_DIM, in_lo, 0.0).astype(BF16)
    hi = jnp.where(lane >= HEAD_DIM, in_hi, 0.0).astype(BF16)
    return lo, hi


def _kv_pads(k, v, kh):
    col = (kh // 2) * LANES
    high = kh % 2 == 1
    return _half_padded(k[:, col:col + LANES], high) + _half_padded(v[:, col:col + LANES], high)


def _attention(items, rows, sink_at, between=lambda: None):
    keys = 2 * WINDOW
    row = lax.broadcasted_iota(jnp.int32, (2 * rows, 1), 0)
    lane = lax.broadcasted_iota(jnp.int32, (2 * rows, LANES), 1)

    scores = []
    for kh, q_at, (k_lo, k_hi, _, _), bias_at, _ in items:
        c0 = 2 * kh * LANES
        qq = jnp.concatenate([q_at(c0), q_at(c0 + LANES)], axis=0).astype(BF16)
        s = lax.dot_general(qq, jnp.concatenate([k_lo, k_hi], axis=0), (((1,), (1,)), ((), ())),
                            preferred_element_type=F32)
        scores.append([s[:, par * keys:(par + 1) * keys] + bias_at(2 * kh + par) for par in range(2)])
        between()

    sinks = [[jnp.where(row < rows, sink_at(4 * kh + par), sink_at(4 * kh + 2 + par)) for par in range(2)]
             for kh, *_ in items]
    maxes = []
    for pair, sks in zip(scores, sinks):
        maxes.append([jnp.maximum(jnp.max(s, axis=-1, keepdims=True), sk) for s, sk in zip(pair, sks)])
        between()
    probs, recips = [], []
    for pair, sks, ms in zip(scores, sinks, maxes):
        probs.append([jnp.exp(s - m) for s, m in zip(pair, ms)])
        recips.append([1.0 / (jnp.sum(p, axis=-1, keepdims=True) + jnp.exp(sk - m))
                       for p, sk, m in zip(probs[-1], sks, ms)])
        between()

    for (kh, _, (_, _, v_lo, v_hi), _, out_ref), pair, rcp in zip(items, probs, recips):
        c0 = 2 * kh * LANES
        p_cat = jnp.concatenate([p.astype(BF16) for p in pair], axis=1)
        o = jnp.dot(p_cat, jnp.concatenate([v_lo, v_hi], axis=0), preferred_element_type=F32)
        o = o * jnp.where(lane < HEAD_DIM, rcp[0], rcp[1])
        out_ref[:, c0:c0 + LANES] = o[:rows].astype(out_ref.dtype)
        out_ref[:, c0 + LANES:c0 + 2 * LANES] = o[rows:].astype(out_ref.dtype)
        between()


def _pool(rows, u_prev, u_cur, pos, wpool_ref, pscale_ref, out_ref):
    ext = jnp.concatenate([u_prev, u_cur], axis=0)
    for gi, w in enumerate(POOL_WINDOWS):
        lo = gi * POOL_GROUP_WIDTH
        e = ext[:, lo:lo + POOL_GROUP_WIDTH]
        s, d = e, 1
        while d < w:
            s = s + pltpu.roll(s, d, axis=0)
            d *= 2
        cnt = jnp.minimum(pos + 1, w).astype(F32)
        z = s[POOL_PREV_ROWS:] / cnt - e[POOL_PREV_ROWS:]
        zz = jnp.dot(z.astype(BF16), wpool_ref[gi], preferred_element_type=F32)
        zz = zz * pscale_ref[:, lo:lo + POOL_GROUP_WIDTH]
        out_ref[:, ATTN_WIDTH + lo:ATTN_WIDTH + lo + POOL_GROUP_WIDTH] = zz.astype(out_ref.dtype)


def _mixer_prompt_kernel(l, tiles_per_seq, n_tiles, sinks_ref, x_ref, xo_ref, gpre_ref, gpost_ref, win_ref, wout_ref,
                         bias_ref, wpool_ref, pscale_ref, kp_any, vp_any, pp_any,
                         out_ref, kp_ref, vp_ref, pp_ref, *scratch):
    del kp_any, vp_any, pp_any
    q_s, kv_s, u_s, mix_s = scratch[0:2], scratch[2:4], scratch[4:6], scratch[6:8]
    y_s, pads_s, utail_s = scratch[8:]
    s = pl.program_id(0)
    blocks = MIX_TM // WINDOW
    block_rows = [pl.ds(j * WINDOW, WINDOW) for j in range(blocks)]

    @pl.when(s == 0)
    def _():
        pads_s[...] = jnp.zeros_like(pads_s)
        utail_s[...] = jnp.zeros_like(utail_s)

    def step(new, project_in, attend, project_out):
        old = 1 - new
        if project_in:
            h = _rms(x_ref[...], gpre_ref[...]).astype(BF16)

        def in_projection_chunk(c):
            r = jnp.dot(h, win_ref[:, c:c + MXU_CHUNK], preferred_element_type=F32)
            if c < ATTN_WIDTH:
                q_s[new][:, c:c + MXU_CHUNK] = r * Q_SCALE
            elif c < ATTN_WIDTH + 2 * KV_WIDTH:
                kv_s[new][:, c - ATTN_WIDTH:c - ATTN_WIDTH + MXU_CHUNK] = r
            else:
                c -= ATTN_WIDTH + 2 * KV_WIDTH
                u_s[new][:, c:c + MXU_CHUNK] = r

        def out_projection_chunk(c):
            y_s[:, c:c + MXU_CHUNK] = jnp.dot(mix_s[new][...], wout_ref[:, c:c + MXU_CHUNK],
                                              preferred_element_type=F32)

        def out_projection_finish():
            out_ref[...] = xo_ref[...] + _rms(y_s[...], gpost_ref[...])

        pending = []
        if project_out:
            pending += [functools.partial(out_projection_chunk, c) for c in range(0, D_MODEL, MXU_CHUNK)]
            pending += [out_projection_finish]
        if project_in:
            pending += [functools.partial(in_projection_chunk, c) for c in range(0, IN_WIDTH, MXU_CHUNK)]
        n_pending = len(pending)
        n_slots = 4 * N_KV_HEADS * blocks
        slots = [0]

        def between():
            slots[0] += 1
            while n_pending - len(pending) < slots[0] * n_pending // n_slots:
                pending.pop(0)()

        if not attend:
            while pending:
                pending.pop(0)()
            if project_in:
                state_outputs(new)
            return

        tile = (s - 1) % tiles_per_seq
        seq_start = tile == 0
        first = seq_start.astype(jnp.int32)
        prev_pads = [pads_s[i] for i in range(KV_PADS * N_KV_HEADS)]
        items = []
        for kh in range(N_KV_HEADS):
            pads = prev_pads[KV_PADS * kh:KV_PADS * (kh + 1)]
            for j, rows in enumerate(block_rows):
                cur = _kv_pads(kv_s[old][rows, :KV_WIDTH], kv_s[old][rows, KV_WIDTH:], kh)
                both = [jnp.concatenate([p, c], axis=0) for p, c in zip(pads, cur)]
                bias_at = (lambda i: bias_ref[first, i]) if j == 0 else (lambda i: bias_ref[0, i])
                items.append((kh, functools.partial(lambda rows, c: q_s[old][rows, c:c + LANES], rows), both,
                              bias_at, mix_s[old].at[rows]))
                pads = cur
            for i, p in enumerate(pads):
                pads_s[KV_PADS * kh + i] = p
        _attention(items, WINDOW, lambda hd: sinks_ref[l, hd], between)
        assert not pending

        u_prev = jnp.where(seq_start, 0.0, utail_s[...])
        for j, rows in enumerate(block_rows):
            u_cur = u_s[old][rows, :]
            pos = (tile * blocks + j) * WINDOW + lax.broadcasted_iota(jnp.int32, (WINDOW, 1), 0)
            _pool(WINDOW, u_prev, u_cur, pos, wpool_ref, pscale_ref, mix_s[old].at[rows])
            u_prev = u_cur[WINDOW - POOL_PREV_ROWS:]
        utail_s[...] = u_prev
        if project_in:
            state_outputs(new)

    def state_outputs(new):
        @pl.when(s % tiles_per_seq == tiles_per_seq - 1)
        def _():
            kp_ref[...] = kv_s[new][MIX_TM - WINDOW:, :KV_WIDTH]
            vp_ref[...] = kv_s[new][MIX_TM - WINDOW:, KV_WIDTH:]
            pp_ref[...] = _last_rows(u_s[new][MIX_TM - 2 * SUBLANES:, :], POOL_STATE)

    pl.when(s == 0)(functools.partial(step, 0, True, False, False))
    pl.when(s == 1)(functools.partial(step, 1, True, True, False))
    for new in range(2):
        pl.when((s >= MIX_LAG) & (s < n_tiles) & (s % 2 == new))(functools.partial(step, new, True, True, True))
    pl.when(s == n_tiles)(functools.partial(step, 0, False, True, True))
    pl.when(s == n_tiles + 1)(functools.partial(step, 1, False, False, True))


def _mixer_prompt(l, x, sinks, gpre, gpost, w_in_b, w_out_b, bias, wpool, pscale, kp, vp, pp, seq):
    m = x.shape[0]
    tm = MIX_TM
    tiles_per_seq = seq // tm
    n_tiles = m // tm
    assert n_tiles % 2 == 0 and n_tiles > MIX_LAG
    cur = lambda s: (jnp.minimum(s, n_tiles - 1), 0)
    lagging = lambda s: (jnp.maximum(s - MIX_LAG, 0), 0)
    state = lambda rows, width: pl.BlockSpec((None, None, rows, width),
                                             lambda s: (l, jnp.minimum(s, n_tiles - 1) // tiles_per_seq, 0, 0))
    return pl.pallas_call(
        functools.partial(_mixer_prompt_kernel, l, tiles_per_seq, n_tiles),
        grid=(n_tiles + MIX_LAG,),
        in_specs=[
            pl.BlockSpec(memory_space=pltpu.SMEM),
            pl.BlockSpec((tm, D_MODEL), cur),
            pl.BlockSpec((tm, D_MODEL), lagging),
            _layer_vec(l, D_MODEL),
            _layer_vec(l, D_MODEL),
            _resident((D_MODEL, IN_WIDTH), lambda s: (0, 0)),
            _resident((MIX_WIDTH, D_MODEL), lambda s: (0, 0)),
            _resident(bias.shape, lambda s: (0, 0, 0, 0)),
            _resident((None,) + wpool.shape[1:], lambda s: (l, 0, 0, 0)),
            _layer_vec(l, POOL_WIDTH),
            pl.BlockSpec(memory_space=pl.ANY),
            pl.BlockSpec(memory_space=pl.ANY),
            pl.BlockSpec(memory_space=pl.ANY),
        ],
        out_specs=[
            pl.BlockSpec((tm, D_MODEL), lagging),
            state(WINDOW, KV_WIDTH),
            state(WINDOW, KV_WIDTH),
            state(POOL_STATE, POOL_WIDTH),
        ],
        out_shape=[
            jax.ShapeDtypeStruct((m, D_MODEL), F32),
            jax.ShapeDtypeStruct(kp.shape, F32),
            jax.ShapeDtypeStruct(vp.shape, F32),
            jax.ShapeDtypeStruct(pp.shape, F32),
        ],
        scratch_shapes=[
            pltpu.VMEM((tm, ATTN_WIDTH), F32), pltpu.VMEM((tm, ATTN_WIDTH), F32),
            pltpu.VMEM((tm, 2 * KV_WIDTH), F32), pltpu.VMEM((tm, 2 * KV_WIDTH), F32),
            pltpu.VMEM((tm, POOL_WIDTH), F32), pltpu.VMEM((tm, POOL_WIDTH), F32),
            pltpu.VMEM((tm, MIX_WIDTH), BF16), pltpu.VMEM((tm, MIX_WIDTH), BF16),
            pltpu.VMEM((tm, D_MODEL), F32),
            pltpu.VMEM((KV_PADS * N_KV_HEADS, WINDOW, LANES), BF16),
            pltpu.VMEM((POOL_PREV_ROWS, POOL_WIDTH), F32),
        ],
        input_output_aliases={10: 1, 11: 2, 12: 3},
        compiler_params=_params("arbitrary"),
        name="mixer_prompt",
    )(sinks, x, x, gpre, gpost, w_in_b, w_out_b, bias, wpool, pscale, kp, vp, pp)


def _inproj_sample_kernel(x_ref, g_ref, w_ref, proj_ref, wb_ref, h_ref):
    j = pl.program_id(0)

    @pl.when(j == 0)
    def _():
        h_ref[...] = _rms(x_ref[...], g_ref[...]).astype(BF16)

    wb = w_ref[...].astype(BF16)
    wb_ref[...] = wb
    p = jnp.dot(h_ref[...], wb, preferred_element_type=F32)
    proj_ref[...] = p * jnp.where(j < ATTN_WIDTH // WEIGHT_BLOCK, Q_SCALE, 1.0)


def _inproj_sample(l, x, g, w):
    m = x.shape[0]
    tn = WEIGHT_BLOCK
    return pl.pallas_call(
        _inproj_sample_kernel,
        grid=(IN_WIDTH // tn,),
        in_specs=[
            pl.BlockSpec((m, D_MODEL), lambda j: (0, 0)),
            _layer_vec(l, D_MODEL),
            pl.BlockSpec((None, D_MODEL, tn), lambda j: (l, 0, j)),
        ],
        out_specs=[
            pl.BlockSpec((m, tn), lambda j: (0, j)),
            pl.BlockSpec((D_MODEL, tn), lambda j: (0, j)),
        ],
        out_shape=[
            jax.ShapeDtypeStruct((m, IN_WIDTH), F32),
            jax.ShapeDtypeStruct((D_MODEL, IN_WIDTH), BF16),
        ],
        scratch_shapes=[pltpu.VMEM((m, D_MODEL), BF16)],
        compiler_params=_params("arbitrary"),
        name="inproj_sample",
    )(x, g, w)


def _shift_in(old, new, n_new):
    r = old.shape[0]
    rolled = pltpu.roll(old, r - n_new, axis=0)
    tail = pltpu.roll(new, SUBLANES - n_new, axis=0)
    row = lax.broadcasted_iota(jnp.int32, (SUBLANES, old.shape[1]), 0)
    last = jnp.where(row < SUBLANES - n_new, rolled[r - SUBLANES:], tail)
    return jnp.concatenate([rolled[:r - SUBLANES], last], axis=0)


def _mixer_sample_kernel(l, n_new, sinks_ref, proj_ref, ck_ref, cv_ref, st_ref, bias_ref, wpool_ref, pscale_ref,
                         ks_any, vs_any, ps_any, out_ref, ko_ref, vo_ref, po_ref):
    del ks_any, vs_any, ps_any
    rows = SAMPLE_ROWS
    pad = jnp.zeros((WINDOW - rows, KV_WIDTH), F32)
    pos = PAST_LEN + lax.broadcasted_iota(jnp.int32, (rows, 1), 0)
    seqs = range(SAMPLE_GROUP)
    k_new = [proj_ref[g, :, ATTN_WIDTH:ATTN_WIDTH + KV_WIDTH] for g in seqs]
    v_new = [proj_ref[g, :, ATTN_WIDTH + KV_WIDTH:ATTN_WIDTH + 2 * KV_WIDTH] for g in seqs]
    kk = [jnp.concatenate([ck_ref[g], k_new[g], pad], axis=0) for g in seqs]
    vv = [jnp.concatenate([cv_ref[g], v_new[g], pad], axis=0) for g in seqs]
    items = [(kh, functools.partial(lambda g, c: proj_ref[g, :, c:c + LANES], g), _kv_pads(kk[g], vv[g], kh),
              lambda i: bias_ref[i], out_ref.at[g]) for kh in range(N_KV_HEADS) for g in seqs]
    _attention(items, rows, lambda hd: sinks_ref[l, hd])
    for g in seqs:
        u_prev, u_cur = st_ref[g], proj_ref[g, :, ATTN_WIDTH + 2 * KV_WIDTH:]
        _pool(rows, u_prev, u_cur, pos, wpool_ref, pscale_ref, out_ref.at[g])
        ko_ref[g] = _shift_in(ck_ref[g], k_new[g], n_new)
        vo_ref[g] = _shift_in(cv_ref[g], v_new[g], n_new)
        po_ref[g] = _last_rows(_shift_in(u_prev, u_cur, n_new), POOL_STATE)


def _mixer_sample(l, proj, cache_k, cache_v, state, sinks, bias, wpool, pscale, ks, vs, ps, n_new):
    n_seq = cache_k.shape[1]
    rows = SAMPLE_ROWS
    grp = SAMPLE_GROUP
    assert n_seq % grp == 0
    per_seq = lambda r, width: pl.BlockSpec((None, grp, r, width), lambda n: (l, n, 0, 0))
    seqs = lambda width: pl.BlockSpec((grp, rows, width), lambda n: (n, 0, 0))
    return pl.pallas_call(
        functools.partial(_mixer_sample_kernel, l, n_new),
        grid=(n_seq // grp,),
        in_specs=[
            pl.BlockSpec(memory_space=pltpu.SMEM),
            seqs(IN_WIDTH),
            per_seq(WINDOW, KV_WIDTH),
            per_seq(WINDOW, KV_WIDTH),
            per_seq(POOL_PREV_ROWS, POOL_WIDTH),
            pl.BlockSpec(bias.shape, lambda n: (0, 0, 0)),
            pl.BlockSpec((None,) + wpool.shape[1:], lambda n: (l, 0, 0, 0)),
            _layer_vec(l, POOL_WIDTH),
            pl.BlockSpec(memory_space=pl.ANY),
            pl.BlockSpec(memory_space=pl.ANY),
            pl.BlockSpec(memory_space=pl.ANY),
        ],
        out_specs=[
            seqs(MIX_WIDTH),
            per_seq(WINDOW, KV_WIDTH),
            per_seq(WINDOW, KV_WIDTH),
            per_seq(POOL_STATE, POOL_WIDTH),
        ],
        out_shape=[
            jax.ShapeDtypeStruct((n_seq, rows, MIX_WIDTH), F32),
            jax.ShapeDtypeStruct(ks.shape, F32),
            jax.ShapeDtypeStruct(vs.shape, F32),
            jax.ShapeDtypeStruct(ps.shape, F32),
        ],
        input_output_aliases={8: 1, 9: 2, 10: 3},
        compiler_params=_params("arbitrary"),
        name="mixer_sample",
    )(sinks, proj.reshape(n_seq, rows, IN_WIDTH), cache_k, cache_v, state, bias, wpool, pscale, ks, vs, ps)


def _outproj_sample_kernel(mix_ref, x_ref, g_ref, w_ref, o_ref, wb_ref):
    k = pl.program_id(0)

    @pl.when(k == 0)
    def _():
        o_ref[...] = jnp.zeros_like(o_ref)

    wb = w_ref[...].astype(BF16)
    wb_ref[...] = wb
    o_ref[...] += jnp.dot(mix_ref[...].astype(BF16), wb, preferred_element_type=F32)

    @pl.when(k == pl.num_programs(0) - 1)
    def _():
        o_ref[...] = x_ref[...] + _rms(o_ref[...], g_ref[...])


def _outproj_sample(l, mix, x, g, w):
    m = x.shape[0]
    tk = WEIGHT_BLOCK
    return pl.pallas_call(
        _outproj_sample_kernel,
        grid=(MIX_WIDTH // tk,),
        in_specs=[
            pl.BlockSpec((m, tk), lambda k: (0, k)),
            pl.BlockSpec((m, D_MODEL), lambda k: (0, 0)),
            _layer_vec(l, D_MODEL),
            pl.BlockSpec((None, tk, D_MODEL), lambda k: (l, k, 0)),
        ],
        out_specs=[
            pl.BlockSpec((m, D_MODEL), lambda k: (0, 0)),
            pl.BlockSpec((tk, D_MODEL), lambda k: (k, 0)),
        ],
        out_shape=[
            jax.ShapeDtypeStruct((m, D_MODEL), F32),
            jax.ShapeDtypeStruct((MIX_WIDTH, D_MODEL), BF16),
        ],
        compiler_params=_params("arbitrary"),
        name="outproj_sample",
    )(mix, x, g, w)


def _rms_chunked(src_ref, inv_ref, finish):
    n = src_ref.shape[0]

    def stats(c, carry):
        rows = pl.ds(pl.multiple_of(c * STAT_ROWS, STAT_ROWS), STAT_ROWS)
        y = src_ref[rows, :]
        inv = lax.rsqrt(jnp.mean(y * y, axis=-1, keepdims=True) + EPS)
        inv_ref[rows, :] = jnp.broadcast_to(inv, (STAT_ROWS, LANES))
        return carry

    def apply(c, carry):
        rows = pl.ds(pl.multiple_of(c * APPLY_ROWS, APPLY_ROWS), APPLY_ROWS)
        inv = jnp.tile(inv_ref[rows, :], (1, src_ref.shape[1] // LANES))
        finish(rows, src_ref[rows, :] * inv)
        return carry

    lax.fori_loop(0, n // STAT_ROWS, stats, 0, unroll=True)
    lax.fori_loop(0, n // APPLY_ROWS, apply, 0)


def _ffn_first_kernel(n_dec, xs_ref, xp_ref, gpre_ref, gpost_ref, wup_ref, wdn_ref,
                      os_ref, op_ref, wupb_ref, wdnb_ref, h_ref, acc_ref, inv_ref):
    f = pl.program_id(0)
    parts = ((xs_ref, os_ref, pl.ds(0, n_dec)), (xp_ref, op_ref, pl.ds(n_dec, FFN_TM)))

    @pl.when(f == 0)
    def _():
        g = gpre_ref[...]
        for x_ref, _, part in parts:
            h_part, acc_part = h_ref.at[part], acc_ref.at[part]

            def store_h(rows, xn, h_part=h_part, acc_part=acc_part):
                h_part[rows, :] = (xn * g).astype(BF16)
                acc_part[rows, :] = jnp.zeros((APPLY_ROWS, D_MODEL), F32)

            _rms_chunked(x_ref, inv_ref.at[part], store_h)

    wupb_ref[...] = wup_ref[...].astype(BF16)
    wdnb_ref[...] = wdn_ref[...].astype(BF16)
    a = jnp.dot(h_ref[...], wupb_ref[...], preferred_element_type=F32)
    a = jnp.square(jnp.maximum(a, 0.0)).astype(BF16)
    acc_ref[...] += jnp.dot(a, wdnb_ref[...], preferred_element_type=F32)

    @pl.when(f == pl.num_programs(0) - 1)
    def _():
        g = gpost_ref[...]
        for x_ref, o_ref, part in parts:

            def store_out(rows, yn, x_ref=x_ref, o_ref=o_ref):
                o_ref[rows, :] = x_ref[rows, :] + yn * g

            _rms_chunked(acc_ref.at[part], inv_ref.at[part], store_out)


def _ffn_first(l, xs, xp, gpre, gpost, wup, wdn):
    n_dec = xs.shape[0]
    m = n_dec + FFN_TM
    tf = FIRST_TF
    return pl.pallas_call(
        functools.partial(_ffn_first_kernel, n_dec),
        grid=(D_FF // tf,),
        in_specs=[
            _resident((n_dec, D_MODEL), lambda f: (0, 0)),
            _resident((FFN_TM, D_MODEL), lambda f: (0, 0)),
            _layer_vec(l, D_MODEL),
            _layer_vec(l, D_MODEL),
            pl.BlockSpec((None, D_MODEL, tf), lambda f: (l, 0, f)),
            pl.BlockSpec((None, tf, D_MODEL), lambda f: (l, f, 0)),
        ],
        out_specs=[
            pl.BlockSpec((n_dec, D_MODEL), lambda f: (0, 0)),
            pl.BlockSpec((FFN_TM, D_MODEL), lambda f: (0, 0)),
            pl.BlockSpec((D_MODEL, tf), lambda f: (0, f)),
            pl.BlockSpec((tf, D_MODEL), lambda f: (f, 0)),
        ],
        out_shape=[
            jax.ShapeDtypeStruct((n_dec, D_MODEL), F32),
            jax.ShapeDtypeStruct((FFN_TM, D_MODEL), F32),
            jax.ShapeDtypeStruct((D_MODEL, D_FF), BF16),
            jax.ShapeDtypeStruct((D_FF, D_MODEL), BF16),
        ],
        scratch_shapes=[pltpu.VMEM((m, D_MODEL), BF16), pltpu.VMEM((m, D_MODEL), F32), pltpu.VMEM((m, LANES), F32)],
        compiler_params=_params("arbitrary"),
        name="ffn_first",
    )(xs, xp, gpre, gpost, wup, wdn)


def _ffn_prompt_kernel(x_ref, first_ref, gpre_ref, gpost_ref, wup_ref, wdn_ref, o_ref, h_ref, inv_ref):
    i, f = pl.program_id(0), pl.program_id(1)

    @pl.when((i == 0) & (f == 0))
    def _():
        o_ref[...] = first_ref[...]

    @pl.when(i > 0)
    def _():
        @pl.when(f == 0)
        def _():
            g = gpre_ref[...]

            def store_h(rows, xn):
                h_ref[rows, :] = (xn * g).astype(BF16)
                o_ref[rows, :] = jnp.zeros((APPLY_ROWS, D_MODEL), F32)

            _rms_chunked(x_ref, inv_ref, store_h)

        a = jnp.dot(h_ref[...], wup_ref[...], preferred_element_type=F32)
        a = jnp.square(jnp.maximum(a, 0.0)).astype(BF16)
        o_ref[...] += jnp.dot(a, wdn_ref[...], preferred_element_type=F32)

        @pl.when(f == pl.num_programs(1) - 1)
        def _():
            g = gpost_ref[...]

            def store_out(rows, yn):
                o_ref[rows, :] = x_ref[rows, :] + yn * g

            _rms_chunked(o_ref, inv_ref, store_out)


def _ffn_prompt(l, x, first, gpre, gpost, wup, wdn):
    m = x.shape[0]
    tm, tf = FFN_TM, FFN_TF
    block = lambda i, f: jnp.where(i == 0, 0, f)
    return pl.pallas_call(
        _ffn_prompt_kernel,
        grid=(m // tm, D_FF // tf),
        in_specs=[
            pl.BlockSpec((tm, D_MODEL), lambda i, f: (i, 0)),
            _resident((tm, D_MODEL), lambda i, f: (0, 0)),
            _layer_vec(l, D_MODEL),
            _layer_vec(l, D_MODEL),
            pl.BlockSpec((D_MODEL, tf), lambda i, f: (0, block(i, f))),
            pl.BlockSpec((tf, D_MODEL), lambda i, f: (block(i, f), 0)),
        ],
        out_specs=pl.BlockSpec((tm, D_MODEL), lambda i, f: (i, 0)),
        out_shape=jax.ShapeDtypeStruct((m, D_MODEL), F32),
        scratch_shapes=[pltpu.VMEM((tm, D_MODEL), BF16), pltpu.VMEM((tm, LANES), F32)],
        compiler_params=_params("arbitrary", "arbitrary"),
        name="ffn_prompt",
    )(x, first, gpre, gpost, wup, wdn)


def _bias_tables():
    heads = np.arange(1, N_Q_HEADS + 1, dtype=np.float32)
    slopes = np.exp2(np.float32(-8.0) * heads / np.float32(N_Q_HEADS)).astype(np.float32)

    def table(rows, mask_prev):
        i = np.arange(rows)[:, None]
        j = np.arange(2 * WINDOW)[None, :]
        dist = i + WINDOW - j
        valid = (dist >= 0) & (dist < WINDOW)
        if mask_prev:
            valid = valid & (j >= WINDOW)
        out = np.empty((2 * N_KV_HEADS, 2 * rows, 2 * WINDOW), np.float32)
        for kh in range(N_KV_HEADS):
            for par in range(2):
                for half, head in enumerate((4 * kh + par, 4 * kh + 2 + par)):
                    bias = (-slopes[head]) * dist.astype(np.float32)
                    out[2 * kh + par, half * rows:(half + 1) * rows] = np.where(valid, bias, np.float32(NEG_BIG))
        return out

    prompt = np.stack([table(WINDOW, False), table(WINDOW, True)])
    sample = table(SAMPLE_ROWS, False)
    return jnp.asarray(prompt), jnp.asarray(sample)


def kernel(x_prompt, x_sample, cache_k, cache_v, state_pool, w_in, w_out, w_pool, pool_scale, attn_sinks,
           g_pre_mix, g_post_mix, g_pre_ffn, g_post_ffn, w_up, w_down):
    n_seq, seq, _ = x_prompt.shape
    dec_batch, dec_seq, _ = x_sample.shape
    assert seq % MIX_TM == 0 and (n_seq * seq) % FFN_TM == 0 and dec_seq <= SAMPLE_ROWS
    bias_prompt, bias_sample = _bias_tables()

    vec = lambda p: p.reshape(DEPTH, 1, p.shape[-1])
    gpm, gqm, gpf, gqf, pscale = vec(g_pre_mix), vec(g_post_mix), vec(g_pre_ffn), vec(g_post_ffn), vec(pool_scale)
    w_pool_b = w_pool.astype(BF16)

    xp = x_prompt.reshape(n_seq * seq, D_MODEL)
    xs = jnp.pad(x_sample, ((0, 0), (0, SAMPLE_ROWS - dec_seq), (0, 0))).reshape(dec_batch * SAMPLE_ROWS, D_MODEL)
    ck_all = cache_k.reshape(DEPTH, dec_batch, WINDOW, KV_WIDTH)
    cv_all = cache_v.reshape(DEPTH, dec_batch, WINDOW, KV_WIDTH)
    st_all = jnp.pad(state_pool, ((0, 0), (0, 0), (1, 0), (0, 0)))

    kp = jnp.zeros((DEPTH, n_seq, WINDOW, KV_WIDTH), F32)
    vp = jnp.zeros((DEPTH, n_seq, WINDOW, KV_WIDTH), F32)
    pp = jnp.zeros((DEPTH, n_seq, POOL_STATE, POOL_WIDTH), F32)
    ks = jnp.zeros((DEPTH, dec_batch, WINDOW, KV_WIDTH), F32)
    vs = jnp.zeros((DEPTH, dec_batch, WINDOW, KV_WIDTH), F32)
    ps = jnp.zeros((DEPTH, dec_batch, POOL_STATE, POOL_WIDTH), F32)

    for l in range(DEPTH):
        proj, w_in_b = _inproj_sample(l, xs, gpm, w_in)
        mix, ks, vs, ps = _mixer_sample(l, proj, ck_all, cv_all, st_all, attn_sinks, bias_sample, w_pool_b, pscale,
                                        ks, vs, ps, dec_seq)
        xs, w_out_b = _outproj_sample(l, mix.reshape(xs.shape[0], MIX_WIDTH), xs, gqm, w_out)
        xp, kp, vp, pp = _mixer_prompt(l, xp, attn_sinks, gpm, gqm, w_in_b, w_out_b, bias_prompt, w_pool_b, pscale,
                                       kp, vp, pp, seq)
        xs, xp_first, w_up_b, w_down_b = _ffn_first(l, xs, xp, gpf, gqf, w_up, w_down)
        xp = _ffn_prompt(l, xp, xp_first, gpf, gqf, w_up_b, w_down_b)

    y_prompt = xp.reshape(n_seq, seq, D_MODEL)
    y_sample = xs.reshape(dec_batch, SAMPLE_ROWS, D_MODEL)[:, :dec_seq]
    kv_shape = (WINDOW, N_KV_HEADS, HEAD_DIM)
    return (y_prompt, y_sample, kp.reshape((DEPTH, n_seq) + kv_shape), vp.reshape((DEPTH, n_seq) + kv_shape), pp,
            ks.reshape((DEPTH, dec_batch) + kv_shape), vs.reshape((DEPTH, dec_batch) + kv_shape), ps)
```

```python
import functools

import jax
import jax.numpy as jnp
import numpy as np
from jax import lax
from jax.experimental import pallas as pl
from jax.experimental.pallas import tpu as pltpu

D_MODEL = 2048
DEPTH = 4
PAST_LEN = 16384
HEAD_DIM = 64
N_Q_HEADS = 16
N_KV_HEADS = 4
ATTN_WIDTH = N_Q_HEADS * HEAD_DIM
KV_WIDTH = N_KV_HEADS * HEAD_DIM
WINDOW = 128
POOL_WINDOWS = (2, 4, 8, 16)
POOL_GROUP_WIDTH = 256
POOL_WIDTH = len(POOL_WINDOWS) * POOL_GROUP_WIDTH
POOL_STATE = max(POOL_WINDOWS) - 1
POOL_PREV_ROWS = POOL_STATE + 1
MIX_WIDTH = ATTN_WIDTH + POOL_WIDTH
IN_WIDTH = ATTN_WIDTH + 2 * KV_WIDTH + POOL_WIDTH
D_FF = 4 * D_MODEL
EPS = 1e-6
Q_SCALE = HEAD_DIM ** -0.5

LANES = 128
SUBLANES = 8
NEG_BIG = -1e30
VMEM_LIMIT_BYTES = 56 * 1024 * 1024
SAMPLE_ROWS = SUBLANES

FFN_TM = 512
FFN_TF = 1024
STAT_ROWS = 256
APPLY_ROWS = 64
SAMPLE_GROUP = 4
WEIGHT_BLOCK = 512
MIX_TM = 256
MXU_CHUNK = 256
MIX_LAG = 2
KV_PADS = 4

BF16 = jnp.bfloat16
F32 = jnp.float32


def _rms(x, g):
    ms = jnp.mean(x * x, axis=-1, keepdims=True)
    return (x * lax.rsqrt(ms + EPS)) * g


def _params(*semantics):
    return pltpu.CompilerParams(dimension_semantics=semantics, vmem_limit_bytes=VMEM_LIMIT_BYTES)


def _layer_vec(l, width):
    return pl.BlockSpec((None, 1, width), lambda *_: (l, 0, 0))


def _resident(shape, index_map):
    return pl.BlockSpec(shape, index_map, pipeline_mode=pl.Buffered(1))


def _last_rows(x, n):
    tail = x[x.shape[0] - 2 * SUBLANES:]
    return pltpu.roll(tail, n, axis=0)[:n]


def _half_padded(x2, head_in_high_half):
    lane = lax.broadcasted_iota(jnp.int32, x2.shape, 1)
    swapped = pltpu.roll(x2, HEAD_DIM, axis=1)
    in_lo, in_hi = (swapped, x2) if head_in_high_half else (x2, swapped)
    lo = jnp.where(lane < HEAD_DIM, in_lo, 0.0).astype(BF16)
    hi = jnp.where(lane >= HEAD_DIM, in_hi, 0.0).astype(BF16)
    return lo, hi


def _kv_pads(k, v, kh):
    col = (kh // 2) * LANES
    high = kh % 2 == 1
    return _half_padded(k[:, col:col + LANES], high) + _half_padded(v[:, col:col + LANES], high)


def _attention(items, rows, sink_at, between=lambda: None):
    keys = 2 * WINDOW
    row = lax.broadcasted_iota(jnp.int32, (2 * rows, 1), 0)
    lane = lax.broadcasted_iota(jnp.int32, (2 * rows, LANES), 1)

    scores = []
    for kh, q_at, (k_lo, k_hi, _, _), bias_at, _ in items:
        c0 = 2 * kh * LANES
        qq = jnp.concatenate([q_at(c0), q_at(c0 + LANES)], axis=0).astype(BF16)
        s = lax.dot_general(qq, jnp.concatenate([k_lo, k_hi], axis=0), (((1,), (1,)), ((), ())),
                            preferred_element_type=F32)
        scores.append([s[:, par * keys:(par + 1) * keys] + bias_at(2 * kh + par) for par in range(2)])
        between()

    sinks = [[jnp.where(row < rows, sink_at(4 * kh + par), sink_at(4 * kh + 2 + par)) for par in range(2)]
             for kh, *_ in items]
    maxes = []
    for pair, sks in zip(scores, sinks):
        maxes.append([jnp.maximum(jnp.max(s, axis=-1, keepdims=True), sk) for s, sk in zip(pair, sks)])
        between()
    probs, recips = [], []
    for pair, sks, ms in zip(scores, sinks, maxes):
        probs.append([jnp.exp(s - m) for s, m in zip(pair, ms)])
        recips.append([1.0 / (jnp.sum(p, axis=-1, keepdims=True) + jnp.exp(sk - m))
                       for p, sk, m in zip(probs[-1], sks, ms)])
        between()

    for (kh, _, (_, _, v_lo, v_hi), _, out_ref), pair, rcp in zip(items, probs, recips):
        c0 = 2 * kh * LANES
        p_cat = jnp.concatenate([p.astype(BF16) for p in pair], axis=1)
        o = jnp.dot(p_cat, jnp.concatenate([v_lo, v_hi], axis=0), preferred_element_type=F32)
        o = o * jnp.where(lane < HEAD_DIM, rcp[0], rcp[1])
        out_ref[:, c0:c0 + LANES] = o[:rows].astype(out_ref.dtype)
        out_ref[:, c0 + LANES:c0 + 2 * LANES] = o[rows:].astype(out_ref.dtype)
        between()


def _pool(rows, u_prev, u_cur, pos, wpool_ref, pscale_ref, out_ref):
    ext = jnp.concatenate([u_prev, u_cur], axis=0)
    for gi, w in enumerate(POOL_WINDOWS):
        lo = gi * POOL_GROUP_WIDTH
        e = ext[:, lo:lo + POOL_GROUP_WIDTH]
        s, d = e, 1
        while d < w:
            s = s + pltpu.roll(s, d, axis=0)
            d *= 2
        cnt = jnp.minimum(pos + 1, w).astype(F32)
        z = s[POOL_PREV_ROWS:] / cnt - e[POOL_PREV_ROWS:]
        zz = jnp.dot(z.astype(BF16), wpool_ref[gi], preferred_element_type=F32)
        zz = zz * pscale_ref[:, lo:lo + POOL_GROUP_WIDTH]
        out_ref[:, ATTN_WIDTH + lo:ATTN_WIDTH + lo + POOL_GROUP_WIDTH] = zz.astype(out_ref.dtype)


def _mixer_prompt_kernel(l, tiles_per_seq, n_tiles, sinks_ref, x_ref, xo_ref, gpre_ref, gpost_ref, win_ref, wout_ref,
                         bias_ref, wpool_ref, pscale_ref, kp_any, vp_any, pp_any,
                         out_ref, kp_ref, vp_ref, pp_ref, *scratch):
    del kp_any, vp_any, pp_any
    q_s, kv_s, u_s, mix_s = scratch[0:2], scratch[2:4], scratch[4:6], scratch[6:8]
    y_s, pads_s, utail_s = scratch[8:]
    s = pl.program_id(0)
    blocks = MIX_TM // WINDOW
    block_rows = [pl.ds(j * WINDOW, WINDOW) for j in range(blocks)]

    @pl.when(s == 0)
    def _():
        pads_s[...] = jnp.zeros_like(pads_s)
        utail_s[...] = jnp.zeros_like(utail_s)

    def step(new, project_in, attend, project_out):
        old = 1 - new
        if project_in:
            h = _rms(x_ref[...], gpre_ref[...]).astype(BF16)

        def in_projection_chunk(c):
            r = jnp.dot(h, win_ref[:, c:c + MXU_CHUNK], preferred_element_type=F32)
            if c < ATTN_WIDTH:
                q_s[new][:, c:c + MXU_CHUNK] = r * Q_SCALE
            elif c < ATTN_WIDTH + 2 * KV_WIDTH:
                kv_s[new][:, c - ATTN_WIDTH:c - ATTN_WIDTH + MXU_CHUNK] = r
            else:
                c -= ATTN_WIDTH + 2 * KV_WIDTH
                u_s[new][:, c:c + MXU_CHUNK] = r

        def out_projection_chunk(c):
            y_s[:, c:c + MXU_CHUNK] = jnp.dot(mix_s[new][...], wout_ref[:, c:c + MXU_CHUNK],
                                              preferred_element_type=F32)

        def out_projection_finish():
            out_ref[...] = xo_ref[...] + _rms(y_s[...], gpost_ref[...])

        pending = []
        if project_out:
            pending += [functools.partial(out_projection_chunk, c) for c in range(0, D_MODEL, MXU_CHUNK)]
            pending += [out_projection_finish]
        if project_in:
            pending += [functools.partial(in_projection_chunk, c) for c in range(0, IN_WIDTH, MXU_CHUNK)]
        n_pending = len(pending)
        n_slots = 4 * N_KV_HEADS * blocks
        slots = [0]

        def between():
            slots[0] += 1
            while n_pending - len(pending) < slots[0] * n_pending // n_slots:
                pending.pop(0)()

        if not attend:
            while pending:
                pending.pop(0)()
            if project_in:
                state_outputs(new)
            return

        tile = (s - 1) % tiles_per_seq
        seq_start = tile == 0
        first = seq_start.astype(jnp.int32)
        prev_pads = [pads_s[i] for i in range(KV_PADS * N_KV_HEADS)]
        items = []
        for kh in range(N_KV_HEADS):
            pads = prev_pads[KV_PADS * kh:KV_PADS * (kh + 1)]
            for j, rows in enumerate(block_rows):
                cur = _kv_pads(kv_s[old][rows, :KV_WIDTH], kv_s[old][rows, KV_WIDTH:], kh)
                both = [jnp.concatenate([p, c], axis=0) for p, c in zip(pads, cur)]
                bias_at = (lambda i: bias_ref[first, i]) if j == 0 else (lambda i: bias_ref[0, i])
                items.append((kh, functools.partial(lambda rows, c: q_s[old][rows, c:c + LANES], rows), both,
                              bias_at, mix_s[old].at[rows]))
                pads = cur
            for i, p in enumerate(pads):
                pads_s[KV_PADS * kh + i] = p
        _attention(items, WINDOW, lambda hd: sinks_ref[l, hd], between)
        assert not pending

        u_prev = jnp.where(seq_start, 0.0, utail_s[...])
        for j, rows in enumerate(block_rows):
            u_cur = u_s[old][rows, :]
            pos = (tile * blocks + j) * WINDOW + lax.broadcasted_iota(jnp.int32, (WINDOW, 1), 0)
            _pool(WINDOW, u_prev, u_cur, pos, wpool_ref, pscale_ref, mix_s[old].at[rows])
            u_prev = u_cur[WINDOW - POOL_PREV_ROWS:]
        utail_s[...] = u_prev
        if project_in:
            state_outputs(new)

    def state_outputs(new):
        @pl.when(s % tiles_per_seq == tiles_per_seq - 1)
        def _():
            kp_ref[...] = kv_s[new][MIX_TM - WINDOW:, :KV_WIDTH]
            vp_ref[...] = kv_s[new][MIX_TM - WINDOW:, KV_WIDTH:]
            pp_ref[...] = _last_rows(u_s[new][MIX_TM - 2 * SUBLANES:, :], POOL_STATE)

    pl.when(s == 0)(functools.partial(step, 0, True, False, False))
    pl.when(s == 1)(functools.partial(step, 1, True, True, False))
    for new in range(2):
        pl.when((s >= MIX_LAG) & (s < n_tiles) & (s % 2 == new))(functools.partial(step, new, True, True, True))
    pl.when(s == n_tiles)(functools.partial(step, 0, False, True, True))
    pl.when(s == n_tiles + 1)(functools.partial(step, 1, False, False, True))


def _mixer_prompt(l, x, sinks, gpre, gpost, w_in_b, w_out_b, bias, wpool, pscale, kp, vp, pp, seq):
    m = x.shape[0]
    tm = MIX_TM
    tiles_per_seq = seq // tm
    n_tiles = m // tm
    assert n_tiles % 2 == 0 and n_tiles > MIX_LAG
    cur = lambda s: (jnp.minimum(s, n_tiles - 1), 0)
    lagging = lambda s: (jnp.maximum(s - MIX_LAG, 0), 0)
    state = lambda rows, width: pl.BlockSpec((None, None, rows, width),
                                             lambda s: (l, jnp.minimum(s, n_tiles - 1) // tiles_per_seq, 0, 0))
    return pl.pallas_call(
        functools.partial(_mixer_prompt_kernel, l, tiles_per_seq, n_tiles),
        grid=(n_tiles + MIX_LAG,),
        in_specs=[
            pl.BlockSpec(memory_space=pltpu.SMEM),
            pl.BlockSpec((tm, D_MODEL), cur),
            pl.BlockSpec((tm, D_MODEL), lagging),
            _layer_vec(l, D_MODEL),
            _layer_vec(l, D_MODEL),
            _resident((D_MODEL, IN_WIDTH), lambda s: (0, 0)),
            _resident((MIX_WIDTH, D_MODEL), lambda s: (0, 0)),
            _resident(bias.shape, lambda s: (0, 0, 0, 0)),
            _resident((None,) + wpool.shape[1:], lambda s: (l, 0, 0, 0)),
            _layer_vec(l, POOL_WIDTH),
            pl.BlockSpec(memory_space=pl.ANY),
            pl.BlockSpec(memory_space=pl.ANY),
            pl.BlockSpec(memory_space=pl.ANY),
        ],
        out_specs=[
            pl.BlockSpec((tm, D_MODEL), lagging),
            state(WINDOW, KV_WIDTH),
            state(WINDOW, KV_WIDTH),
            state(POOL_STATE, POOL_WIDTH),
        ],
        out_shape=[
            jax.ShapeDtypeStruct((m, D_MODEL), F32),
            jax.ShapeDtypeStruct(kp.shape, F32),
            jax.ShapeDtypeStruct(vp.shape, F32),
            jax.ShapeDtypeStruct(pp.shape, F32),
        ],
        scratch_shapes=[
            pltpu.VMEM((tm, ATTN_WIDTH), F32), pltpu.VMEM((tm, ATTN_WIDTH), F32),
            pltpu.VMEM((tm, 2 * KV_WIDTH), F32), pltpu.VMEM((tm, 2 * KV_WIDTH), F32),
            pltpu.VMEM((tm, POOL_WIDTH), F32), pltpu.VMEM((tm, POOL_WIDTH), F32),
            pltpu.VMEM((tm, MIX_WIDTH), BF16), pltpu.VMEM((tm, MIX_WIDTH), BF16),
            pltpu.VMEM((tm, D_MODEL), F32),
            pltpu.VMEM((KV_PADS * N_KV_HEADS, WINDOW, LANES), BF16),
            pltpu.VMEM((POOL_PREV_ROWS, POOL_WIDTH), F32),
        ],
        input_output_aliases={10: 1, 11: 2, 12: 3},
        compiler_params=_params("arbitrary"),
        name="mixer_prompt",
    )(sinks, x, x, gpre, gpost, w_in_b, w_out_b, bias, wpool, pscale, kp, vp, pp)


def _inproj_sample_kernel(x_ref, g_ref, w_ref, proj_ref, wb_ref, h_ref):
    j = pl.program_id(0)

    @pl.when(j == 0)
    def _():
        h_ref[...] = _rms(x_ref[...], g_ref[...]).astype(BF16)

    wb = w_ref[...].astype(BF16)
    wb_ref[...] = wb
    p = jnp.dot(h_ref[...], wb, preferred_element_type=F32)
    proj_ref[...] = p * jnp.where(j < ATTN_WIDTH // WEIGHT_BLOCK, Q_SCALE, 1.0)


def _inproj_sample(l, x, g, w):
    m = x.shape[0]
    tn = WEIGHT_BLOCK
    return pl.pallas_call(
        _inproj_sample_kernel,
        grid=(IN_WIDTH // tn,),
        in_specs=[
            pl.BlockSpec((m, D_MODEL), lambda j: (0, 0)),
            _layer_vec(l, D_MODEL),
            pl.BlockSpec((None, D_MODEL, tn), lambda j: (l, 0, j)),
        ],
        out_specs=[
            pl.BlockSpec((m, tn), lambda j: (0, j)),
            pl.BlockSpec((D_MODEL, tn), lambda j: (0, j)),
        ],
        out_shape=[
            jax.ShapeDtypeStruct((m, IN_WIDTH), F32),
            jax.ShapeDtypeStruct((D_MODEL, IN_WIDTH), BF16),
        ],
        scratch_shapes=[pltpu.VMEM((m, D_MODEL), BF16)],
        compiler_params=_params("arbitrary"),
        name="inproj_sample",
    )(x, g, w)


def _shift_in(old, new, n_new):
    r = old.shape[0]
    rolled = pltpu.roll(old, r - n_new, axis=0)
    tail = pltpu.roll(new, SUBLANES - n_new, axis=0)
    row = lax.broadcasted_iota(jnp.int32, (SUBLANES, old.shape[1]), 0)
    last = jnp.where(row < SUBLANES - n_new, rolled[r - SUBLANES:], tail)
    return jnp.concatenate([rolled[:r - SUBLANES], last], axis=0)


def _mixer_sample_kernel(l, n_new, sinks_ref, proj_ref, ck_ref, cv_ref, st_ref, bias_ref, wpool_ref, pscale_ref,
                         ks_any, vs_any, ps_any, out_ref, ko_ref, vo_ref, po_ref):
    del ks_any, vs_any, ps_any
    rows = SAMPLE_ROWS
    pad = jnp.zeros((WINDOW - rows, KV_WIDTH), F32)
    pos = PAST_LEN + lax.broadcasted_iota(jnp.int32, (rows, 1), 0)
    seqs = range(SAMPLE_GROUP)
    k_new = [proj_ref[g, :, ATTN_WIDTH:ATTN_WIDTH + KV_WIDTH] for g in seqs]
    v_new = [proj_ref[g, :, ATTN_WIDTH + KV_WIDTH:ATTN_WIDTH + 2 * KV_WIDTH] for g in seqs]
    kk = [jnp.concatenate([ck_ref[g], k_new[g], pad], axis=0) for g in seqs]
    vv = [jnp.concatenate([cv_ref[g], v_new[g], pad], axis=0) for g in seqs]
    items = [(kh, functools.partial(lambda g, c: proj_ref[g, :, c:c + LANES], g), _kv_pads(kk[g], vv[g], kh),
              lambda i: bias_ref[i], out_ref.at[g]) for kh in range(N_KV_HEADS) for g in seqs]
    _attention(items, rows, lambda hd: sinks_ref[l, hd])
    for g in seqs:
        u_prev, u_cur = st_ref[g], proj_ref[g, :, ATTN_WIDTH + 2 * KV_WIDTH:]
        _pool(rows, u_prev, u_cur, pos, wpool_ref, pscale_ref, out_ref.at[g])
        ko_ref[g] = _shift_in(ck_ref[g], k_new[g], n_new)
        vo_ref[g] = _shift_in(cv_ref[g], v_new[g], n_new)
        po_ref[g] = _last_rows(_shift_in(u_prev, u_cur, n_new), POOL_STATE)


def _mixer_sample(l, proj, cache_k, cache_v, state, sinks, bias, wpool, pscale, ks, vs, ps, n_new):
    n_seq = cache_k.shape[1]
    rows = SAMPLE_ROWS
    grp = SAMPLE_GROUP
    assert n_seq % grp == 0
    per_seq = lambda r, width: pl.BlockSpec((None, grp, r, width), lambda n: (l, n, 0, 0))
    seqs = lambda width: pl.BlockSpec((grp, rows, width), lambda n: (n, 0, 0))
    return pl.pallas_call(
        functools.partial(_mixer_sample_kernel, l, n_new),
        grid=(n_seq // grp,),
        in_specs=[
            pl.BlockSpec(memory_space=pltpu.SMEM),
            seqs(IN_WIDTH),
            per_seq(WINDOW, KV_WIDTH),
            per_seq(WINDOW, KV_WIDTH),
            per_seq(POOL_PREV_ROWS, POOL_WIDTH),
            pl.BlockSpec(bias.shape, lambda n: (0, 0, 0)),
            pl.BlockSpec((None,) + wpool.shape[1:], lambda n: (l, 0, 0, 0)),
            _layer_vec(l, POOL_WIDTH),
            pl.BlockSpec(memory_space=pl.ANY),
            pl.BlockSpec(memory_space=pl.ANY),
            pl.BlockSpec(memory_space=pl.ANY),
        ],
        out_specs=[
            seqs(MIX_WIDTH),
            per_seq(WINDOW, KV_WIDTH),
            per_seq(WINDOW, KV_WIDTH),
            per_seq(POOL_STATE, POOL_WIDTH),
        ],
        out_shape=[
            jax.ShapeDtypeStruct((n_seq, rows, MIX_WIDTH), F32),
            jax.ShapeDtypeStruct(ks.shape, F32),
            jax.ShapeDtypeStruct(vs.shape, F32),
            jax.ShapeDtypeStruct(ps.shape, F32),
        ],
        input_output_aliases={8: 1, 9: 2, 10: 3},
        compiler_params=_params("arbitrary"),
        name="mixer_sample",
    )(sinks, proj.reshape(n_seq, rows, IN_WIDTH), cache_k, cache_v, state, bias, wpool, pscale, ks, vs, ps)


def _outproj_sample_kernel(mix_ref, x_ref, g_ref, w_ref, o_ref, wb_ref):
    k = pl.program_id(0)

    @pl.when(k == 0)
    def _():
        o_ref[...] = jnp.zeros_like(o_ref)

    wb = w_ref[...].astype(BF16)
    wb_ref[...] = wb
    o_ref[...] += jnp.dot(mix_ref[...].astype(BF16), wb, preferred_element_type=F32)

    @pl.when(k == pl.num_programs(0) - 1)
    def _():
        o_ref[...] = x_ref[...] + _rms(o_ref[...], g_ref[...])


def _outproj_sample(l, mix, x, g, w):
    m = x.shape[0]
    tk = WEIGHT_BLOCK
    return pl.pallas_call(
        _outproj_sample_kernel,
        grid=(MIX_WIDTH // tk,),
        in_specs=[
            pl.BlockSpec((m, tk), lambda k: (0, k)),
            pl.BlockSpec((m, D_MODEL), lambda k: (0, 0)),
            _layer_vec(l, D_MODEL),
            pl.BlockSpec((None, tk, D_MODEL), lambda k: (l, k, 0)),
        ],
        out_specs=[
            pl.BlockSpec((m, D_MODEL), lambda k: (0, 0)),
            pl.BlockSpec((tk, D_MODEL), lambda k: (k, 0)),
        ],
        out_shape=[
            jax.ShapeDtypeStruct((m, D_MODEL), F32),
            jax.ShapeDtypeStruct((MIX_WIDTH, D_MODEL), BF16),
        ],
        compiler_params=_params("arbitrary"),
        name="outproj_sample",
    )(mix, x, g, w)


def _rms_chunked(src_ref, inv_ref, finish):
    n = src_ref.shape[0]

    def stats(c, carry):
        rows = pl.ds(pl.multiple_of(c * STAT_ROWS, STAT_ROWS), STAT_ROWS)
        y = src_ref[rows, :]
        inv = lax.rsqrt(jnp.mean(y * y, axis=-1, keepdims=True) + EPS)
        inv_ref[rows, :] = jnp.broadcast_to(inv, (STAT_ROWS, LANES))
        return carry

    def apply(c, carry):
        rows = pl.ds(pl.multiple_of(c * APPLY_ROWS, APPLY_ROWS), APPLY_ROWS)
        inv = jnp.tile(inv_ref[rows, :], (1, src_ref.shape[1] // LANES))
        finish(rows, src_ref[rows, :] * inv)
        return carry

    lax.fori_loop(0, n // STAT_ROWS, stats, 0, unroll=True)
    lax.fori_loop(0, n // APPLY_ROWS, apply, 0)


def _ffn_kernel(cast_weights, x_ref, gpre_ref, gpost_ref, wup_ref, wdn_ref, o_ref, *rest):
    f = pl.program_id(1)
    h_ref, inv_ref = rest[-2:]

    @pl.when(f == 0)
    def _():
        g = gpre_ref[...]

        def store_h(rows, xn):
            h_ref[rows, :] = (xn * g).astype(BF16)

        _rms_chunked(x_ref, inv_ref, store_h)

    if cast_weights:
        wupb_ref, wdnb_ref = rest[:2]
        wupb_ref[...] = wup_ref[...].astype(BF16)
        wdnb_ref[...] = wdn_ref[...].astype(BF16)
        wup_ref, wdn_ref = wupb_ref, wdnb_ref

    def block(first):
        a = jnp.dot(h_ref[...], wup_ref[...], preferred_element_type=F32)
        a = jnp.square(jnp.maximum(a, 0.0)).astype(BF16)
        d = jnp.dot(a, wdn_ref[...], preferred_element_type=F32)
        if first:
            o_ref[...] = d
        else:
            o_ref[...] += d

    pl.when(f == 0)(functools.partial(block, True))
    pl.when(f > 0)(functools.partial(block, False))

    @pl.when(f == pl.num_programs(1) - 1)
    def _():
        g = gpost_ref[...]

        def store_out(rows, yn):
            o_ref[rows, :] = x_ref[rows, :] + yn * g

        _rms_chunked(o_ref, inv_ref, store_out)


def _ffn(l, x, gpre, gpost, wup, wdn, tm, tf, cast_weights):
    m = x.shape[0]
    if cast_weights:
        assert m == tm
        w_specs = [pl.BlockSpec((None, D_MODEL, tf), lambda i, f: (l, 0, f)),
                   pl.BlockSpec((None, tf, D_MODEL), lambda i, f: (l, f, 0))]
        extra_specs = [pl.BlockSpec((D_MODEL, tf), lambda i, f: (0, f)),
                       pl.BlockSpec((tf, D_MODEL), lambda i, f: (f, 0))]
        extra_shapes = [jax.ShapeDtypeStruct((D_MODEL, D_FF), BF16), jax.ShapeDtypeStruct((D_FF, D_MODEL), BF16)]
    else:
        w_specs = [pl.BlockSpec((D_MODEL, tf), lambda i, f: (0, f)),
                   pl.BlockSpec((tf, D_MODEL), lambda i, f: (f, 0))]
        extra_specs, extra_shapes = [], []
    return pl.pallas_call(
        functools.partial(_ffn_kernel, cast_weights),
        grid=(m // tm, D_FF // tf),
        in_specs=[
            pl.BlockSpec((tm, D_MODEL), lambda i, f: (i, 0)),
            _layer_vec(l, D_MODEL),
            _layer_vec(l, D_MODEL),
        ] + w_specs,
        out_specs=[pl.BlockSpec((tm, D_MODEL), lambda i, f: (i, 0))] + extra_specs,
        out_shape=[jax.ShapeDtypeStruct((m, D_MODEL), F32)] + extra_shapes,
        scratch_shapes=[pltpu.VMEM((tm, D_MODEL), BF16), pltpu.VMEM((tm, LANES), F32)],
        compiler_params=_params("arbitrary", "arbitrary"),
        name="ffn_sample" if cast_weights else "ffn_prompt",
    )(x, gpre, gpost, wup, wdn)


def _bias_tables():
    heads = np.arange(1, N_Q_HEADS + 1, dtype=np.float32)
    slopes = np.exp2(np.float32(-8.0) * heads / np.float32(N_Q_HEADS)).astype(np.float32)

    def table(rows, mask_prev):
        i = np.arange(rows)[:, None]
        j = np.arange(2 * WINDOW)[None, :]
        dist = i + WINDOW - j
        valid = (dist >= 0) & (dist < WINDOW)
        if mask_prev:
            valid = valid & (j >= WINDOW)
        out = np.empty((2 * N_KV_HEADS, 2 * rows, 2 * WINDOW), np.float32)
        for kh in range(N_KV_HEADS):
            for par in range(2):
                for half, head in enumerate((4 * kh + par, 4 * kh + 2 + par)):
                    bias = (-slopes[head]) * dist.astype(np.float32)
                    out[2 * kh + par, half * rows:(half + 1) * rows] = np.where(valid, bias, np.float32(NEG_BIG))
        return out

    prompt = np.stack([table(WINDOW, False), table(WINDOW, True)])
    sample = table(SAMPLE_ROWS, False)
    return jnp.asarray(prompt), jnp.asarray(sample)


def kernel(x_prompt, x_sample, cache_k, cache_v, state_pool, w_in, w_out, w_pool, pool_scale, attn_sinks,
           g_pre_mix, g_post_mix, g_pre_ffn, g_post_ffn, w_up, w_down):
    n_seq, seq, _ = x_prompt.shape
    dec_batch, dec_seq, _ = x_sample.shape
    assert seq % MIX_TM == 0 and (n_seq * seq) % FFN_TM == 0 and dec_seq <= SAMPLE_ROWS
    bias_prompt, bias_sample = _bias_tables()

    vec = lambda p: p.reshape(DEPTH, 1, p.shape[-1])
    gpm, gqm, gpf, gqf, pscale = vec(g_pre_mix), vec(g_post_mix), vec(g_pre_ffn), vec(g_post_ffn), vec(pool_scale)
    w_pool_b = w_pool.astype(BF16)

    xp = x_prompt.reshape(n_seq * seq, D_MODEL)
    xs = jnp.pad(x_sample, ((0, 0), (0, SAMPLE_ROWS - dec_seq), (0, 0))).reshape(dec_batch * SAMPLE_ROWS, D_MODEL)
    ck_all = cache_k.reshape(DEPTH, dec_batch, WINDOW, KV_WIDTH)
    cv_all = cache_v.reshape(DEPTH, dec_batch, WINDOW, KV_WIDTH)
    st_all = jnp.pad(state_pool, ((0, 0), (0, 0), (1, 0), (0, 0)))

    kp = jnp.zeros((DEPTH, n_seq, WINDOW, KV_WIDTH), F32)
    vp = jnp.zeros((DEPTH, n_seq, WINDOW, KV_WIDTH), F32)
    pp = jnp.zeros((DEPTH, n_seq, POOL_STATE, POOL_WIDTH), F32)
    ks = jnp.zeros((DEPTH, dec_batch, WINDOW, KV_WIDTH), F32)
    vs = jnp.zeros((DEPTH, dec_batch, WINDOW, KV_WIDTH), F32)
    ps = jnp.zeros((DEPTH, dec_batch, POOL_STATE, POOL_WIDTH), F32)

    for l in range(DEPTH):
        proj, w_in_b = _inproj_sample(l, xs, gpm, w_in)
        mix, ks, vs, ps = _mixer_sample(l, proj, ck_all, cv_all, st_all, attn_sinks, bias_sample, w_pool_b, pscale,
                                        ks, vs, ps, dec_seq)
        xs, w_out_b = _outproj_sample(l, mix.reshape(xs.shape[0], MIX_WIDTH), xs, gqm, w_out)
        xs, w_up_b, w_down_b = _ffn(l, xs, gpf, gqf, w_up, w_down, xs.shape[0], WEIGHT_BLOCK, True)

        xp, kp, vp, pp = _mixer_prompt(l, xp, attn_sinks, gpm, gqm, w_in_b, w_out_b, bias_prompt, w_pool_b, pscale,
                                       kp, vp, pp, seq)
        (xp,) = _ffn(l, xp, gpf, gqf, w_up_b, w_down_b, FFN_TM, FFN_TF, False)

    y_prompt = xp.reshape(n_seq, seq, D_MODEL)
    y_sample = xs.reshape(dec_batch, SAMPLE_ROWS, D_MODEL)[:, :dec_seq]
    kv_shape = (WINDOW, N_KV_HEADS, HEAD_DIM)
    return (y_prompt, y_sample, kp.reshape((DEPTH, n_seq) + kv_shape), vp.reshape((DEPTH, n_seq) + kv_shape), pp,
            ks.reshape((DEPTH, dec_batch) + kv_shape), vs.reshape((DEPTH, dec_batch) + kv_shape), ps)
```

```python
import functools

import jax
import jax.numpy as jnp
import numpy as np
from jax import lax
from jax.experimental import pallas as pl
from jax.experimental.pallas import tpu as pltpu

D_MODEL = 2048
DEPTH = 4
PAST_LEN = 16384
HEAD_DIM = 64
N_Q_HEADS = 16
N_KV_HEADS = 4
ATTN_WIDTH = N_Q_HEADS * HEAD_DIM
KV_WIDTH = N_KV_HEADS * HEAD_DIM
WINDOW = 128
POOL_WINDOWS = (2, 4, 8, 16)
POOL_GROUP_WIDTH = 256
POOL_WIDTH = len(POOL_WINDOWS) * POOL_GROUP_WIDTH
POOL_STATE = max(POOL_WINDOWS) - 1
POOL_PREV_ROWS = POOL_STATE + 1
MIX_WIDTH = ATTN_WIDTH + POOL_WIDTH
IN_WIDTH = ATTN_WIDTH + 2 * KV_WIDTH + POOL_WIDTH
D_FF = 4 * D_MODEL
EPS = 1e-6
Q_SCALE = HEAD_DIM ** -0.5

LANES = 128
SUBLANES = 8
NEG_BIG = -1e30
VMEM_LIMIT_BYTES = 56 * 1024 * 1024
SAMPLE_ROWS = SUBLANES

FFN_TM = 1024
FFN_TF = 512
STAT_ROWS = 256
APPLY_ROWS = 64
SAMPLE_GROUP = 8
WEIGHT_BLOCK = 512
MIX_TM = 256
MXU_CHUNK = 256
MIX_LAG = 2
KV_PADS = 4

BF16 = jnp.bfloat16
F32 = jnp.float32


def _rms(x, g):
    ms = jnp.mean(x * x, axis=-1, keepdims=True)
    return (x * lax.rsqrt(ms + EPS)) * g


def _params(*semantics):
    return pltpu.CompilerParams(dimension_semantics=semantics, vmem_limit_bytes=VMEM_LIMIT_BYTES)


def _layer_vec(l, width):
    return pl.BlockSpec((None, 1, width), lambda *_: (l, 0, 0))


def _resident(shape, index_map):
    return pl.BlockSpec(shape, index_map, pipeline_mode=pl.Buffered(1))


def _last_rows(x, n):
    tail = x[x.shape[0] - 2 * SUBLANES:]
    return pltpu.roll(tail, n, axis=0)[:n]


def _half_padded(x2, head_in_high_half):
    lane = lax.broadcasted_iota(jnp.int32, x2.shape, 1)
    swapped = pltpu.roll(x2, HEAD_DIM, axis=1)
    in_lo, in_hi = (swapped, x2) if head_in_high_half else (x2, swapped)
    lo = jnp.where(lane < HEAD_DIM, in_lo, 0.0).astype(BF16)
    hi = jnp.where(lane >= HEAD_DIM, in_hi, 0.0).astype(BF16)
    return lo, hi


def _kv_pads(k, v, kh):
    col = (kh // 2) * LANES
    high = kh % 2 == 1
    return _half_padded(k[:, col:col + LANES], high) + _half_padded(v[:, col:col + LANES], high)


def _attention(items, rows, sink_at, between=lambda: None):
    keys = 2 * WINDOW
    row = lax.broadcasted_iota(jnp.int32, (2 * rows, 1), 0)
    lane = lax.broadcasted_iota(jnp.int32, (2 * rows, LANES), 1)

    scores = []
    for kh, q_at, (k_lo, k_hi, _, _), bias_at, _ in items:
        c0 = 2 * kh * LANES
        qq = jnp.concatenate([q_at(c0), q_at(c0 + LANES)], axis=0).astype(BF16)
        s = lax.dot_general(qq, jnp.concatenate([k_lo, k_hi], axis=0), (((1,), (1,)), ((), ())),
                            preferred_element_type=F32)
        scores.append([s[:, par * keys:(par + 1) * keys] + bias_at(2 * kh + par) for par in range(2)])
        between()

    sinks = [[jnp.where(row < rows, sink_at(4 * kh + par), sink_at(4 * kh + 2 + par)) for par in range(2)]
             for kh, *_ in items]
    maxes = []
    for pair, sks in zip(scores, sinks):
        maxes.append([jnp.maximum(jnp.max(s, axis=-1, keepdims=True), sk) for s, sk in zip(pair, sks)])
        between()
    probs, recips = [], []
    for pair, sks, ms in zip(scores, sinks, maxes):
        probs.append([jnp.exp(s - m) for s, m in zip(pair, ms)])
        recips.append([1.0 / (jnp.sum(p, axis=-1, keepdims=True) + jnp.exp(sk - m))
                       for p, sk, m in zip(probs[-1], sks, ms)])
        between()

    for (kh, _, (_, _, v_lo, v_hi), _, out_ref), pair, rcp in zip(items, probs, recips):
        c0 = 2 * kh * LANES
        p_cat = jnp.concatenate([p.astype(BF16) for p in pair], axis=1)
        o = jnp.dot(p_cat, jnp.concatenate([v_lo, v_hi], axis=0), preferred_element_type=F32)
        o = o * jnp.where(lane < HEAD_DIM, rcp[0], rcp[1])
        out_ref[:, c0:c0 + LANES] = o[:rows].astype(out_ref.dtype)
        out_ref[:, c0 + LANES:c0 + 2 * LANES] = o[rows:].astype(out_ref.dtype)
        between()


def _pool(rows, u_prev, u_cur, pos, wpool_ref, pscale_ref, out_ref):
    ext = jnp.concatenate([u_prev, u_cur], axis=0)
    for gi, w in enumerate(POOL_WINDOWS):
        lo = gi * POOL_GROUP_WIDTH
        e = ext[:, lo:lo + POOL_GROUP_WIDTH]
        s, d = e, 1
        while d < w:
            s = s + pltpu.roll(s, d, axis=0)
            d *= 2
        cnt = jnp.minimum(pos + 1, w).astype(F32)
        z = s[POOL_PREV_ROWS:] / cnt - e[POOL_PREV_ROWS:]
        zz = jnp.dot(z.astype(BF16), wpool_ref[gi], preferred_element_type=F32)
        zz = zz * pscale_ref[:, lo:lo + POOL_GROUP_WIDTH]
        out_ref[:, ATTN_WIDTH + lo:ATTN_WIDTH + lo + POOL_GROUP_WIDTH] = zz.astype(out_ref.dtype)


def _mixer_prompt_kernel(l, tiles_per_seq, n_tiles, sinks_ref, x_ref, xo_ref, gpre_ref, gpost_ref, win_ref, wout_ref,
                         bias_ref, wpool_ref, pscale_ref, kp_any, vp_any, pp_any,
                         out_ref, kp_ref, vp_ref, pp_ref, *scratch):
    del kp_any, vp_any, pp_any
    q_s, kv_s, u_s, mix_s = scratch[0:2], scratch[2:4], scratch[4:6], scratch[6:8]
    y_s, pads_s, utail_s = scratch[8:]
    s = pl.program_id(0)
    blocks = MIX_TM // WINDOW
    block_rows = [pl.ds(j * WINDOW, WINDOW) for j in range(blocks)]

    @pl.when(s == 0)
    def _():
        pads_s[...] = jnp.zeros_like(pads_s)
        utail_s[...] = jnp.zeros_like(utail_s)

    def step(new, project_in, attend, project_out):
        old = 1 - new
        if project_in:
            h = _rms(x_ref[...], gpre_ref[...]).astype(BF16)

        def in_projection_chunk(c):
            r = jnp.dot(h, win_ref[:, c:c + MXU_CHUNK], preferred_element_type=F32)
            if c < ATTN_WIDTH:
                q_s[new][:, c:c + MXU_CHUNK] = r * Q_SCALE
            elif c < ATTN_WIDTH + 2 * KV_WIDTH:
                kv_s[new][:, c - ATTN_WIDTH:c - ATTN_WIDTH + MXU_CHUNK] = r
            else:
                c -= ATTN_WIDTH + 2 * KV_WIDTH
                u_s[new][:, c:c + MXU_CHUNK] = r

        def out_projection_chunk(c):
            y_s[:, c:c + MXU_CHUNK] = jnp.dot(mix_s[new][...], wout_ref[:, c:c + MXU_CHUNK],
                                              preferred_element_type=F32)

        def out_projection_finish():
            out_ref[...] = xo_ref[...] + _rms(y_s[...], gpost_ref[...])

        pending = []
        if project_out:
            pending += [functools.partial(out_projection_chunk, c) for c in range(0, D_MODEL, MXU_CHUNK)]
            pending += [out_projection_finish]
        if project_in:
            pending += [functools.partial(in_projection_chunk, c) for c in range(0, IN_WIDTH, MXU_CHUNK)]
        n_pending = len(pending)
        n_slots = 4 * N_KV_HEADS * blocks
        slots = [0]

        def between():
            slots[0] += 1
            while n_pending - len(pending) < slots[0] * n_pending // n_slots:
                pending.pop(0)()

        if not attend:
            while pending:
                pending.pop(0)()
            if project_in:
                state_outputs(new)
            return

        tile = (s - 1) % tiles_per_seq
        seq_start = tile == 0
        first = seq_start.astype(jnp.int32)
        prev_pads = [pads_s[i] for i in range(KV_PADS * N_KV_HEADS)]
        items = []
        for kh in range(N_KV_HEADS):
            pads = prev_pads[KV_PADS * kh:KV_PADS * (kh + 1)]
            for j, rows in enumerate(block_rows):
                cur = _kv_pads(kv_s[old][rows, :KV_WIDTH], kv_s[old][rows, KV_WIDTH:], kh)
                both = [jnp.concatenate([p, c], axis=0) for p, c in zip(pads, cur)]
                bias_at = (lambda i: bias_ref[first, i]) if j == 0 else (lambda i: bias_ref[0, i])
                items.append((kh, functools.partial(lambda rows, c: q_s[old][rows, c:c + LANES], rows), both,
                              bias_at, mix_s[old].at[rows]))
                pads = cur
            for i, p in enumerate(pads):
                pads_s[KV_PADS * kh + i] = p
        _attention(items, WINDOW, lambda hd: sinks_ref[l, hd], between)
        assert not pending

        u_prev = jnp.where(seq_start, 0.0, utail_s[...])
        for j, rows in enumerate(block_rows):
            u_cur = u_s[old][rows, :]
            pos = (tile * blocks + j) * WINDOW + lax.broadcasted_iota(jnp.int32, (WINDOW, 1), 0)
            _pool(WINDOW, u_prev, u_cur, pos, wpool_ref, pscale_ref, mix_s[old].at[rows])
            u_prev = u_cur[WINDOW - POOL_PREV_ROWS:]
        utail_s[...] = u_prev
        if project_in:
            state_outputs(new)

    def state_outputs(new):
        @pl.when(s % tiles_per_seq == tiles_per_seq - 1)
        def _():
            kp_ref[...] = kv_s[new][MIX_TM - WINDOW:, :KV_WIDTH]
            vp_ref[...] = kv_s[new][MIX_TM - WINDOW:, KV_WIDTH:]
            pp_ref[...] = _last_rows(u_s[new][MIX_TM - 2 * SUBLANES:, :], POOL_STATE)

    pl.when(s == 0)(functools.partial(step, 0, True, False, False))
    pl.when(s == 1)(functools.partial(step, 1, True, True, False))
    for new in range(2):
        pl.when((s >= MIX_LAG) & (s < n_tiles) & (s % 2 == new))(functools.partial(step, new, True, True, True))
    pl.when(s == n_tiles)(functools.partial(step, 0, False, True, True))
    pl.when(s == n_tiles + 1)(functools.partial(step, 1, False, False, True))


def _mixer_prompt(l, x, sinks, gpre, gpost, w_in_b, w_out_b, bias, wpool, pscale, kp, vp, pp, seq):
    m = x.shape[0]
    tm = MIX_TM
    tiles_per_seq = seq // tm
    n_tiles = m // tm
    assert n_tiles % 2 == 0 and n_tiles > MIX_LAG
    cur = lambda s: (jnp.minimum(s, n_tiles - 1), 0)
    lagging = lambda s: (jnp.maximum(s - MIX_LAG, 0), 0)
    state = lambda rows, width: pl.BlockSpec((None, None, rows, width),
                                             lambda s: (l, jnp.minimum(s, n_tiles - 1) // tiles_per_seq, 0, 0))
    return pl.pallas_call(
        functools.partial(_mixer_prompt_kernel, l, tiles_per_seq, n_tiles),
        grid=(n_tiles + MIX_LAG,),
        in_specs=[
            pl.BlockSpec(memory_space=pltpu.SMEM),
            pl.BlockSpec((tm, D_MODEL), cur),
            pl.BlockSpec((tm, D_MODEL), lagging),
            _layer_vec(l, D_MODEL),
            _layer_vec(l, D_MODEL),
            _resident((D_MODEL, IN_WIDTH), lambda s: (0, 0)),
            _resident((MIX_WIDTH, D_MODEL), lambda s: (0, 0)),
            _resident(bias.shape, lambda s: (0, 0, 0, 0)),
            _resident((None,) + wpool.shape[1:], lambda s: (l, 0, 0, 0)),
            _layer_vec(l, POOL_WIDTH),
            pl.BlockSpec(memory_space=pl.ANY),
            pl.BlockSpec(memory_space=pl.ANY),
            pl.BlockSpec(memory_space=pl.ANY),
        ],
        out_specs=[
            pl.BlockSpec((tm, D_MODEL), lagging),
            state(WINDOW, KV_WIDTH),
            state(WINDOW, KV_WIDTH),
            state(POOL_STATE, POOL_WIDTH),
        ],
        out_shape=[
            jax.ShapeDtypeStruct((m, D_MODEL), F32),
            jax.ShapeDtypeStruct(kp.shape, F32),
            jax.ShapeDtypeStruct(vp.shape, F32),
            jax.ShapeDtypeStruct(pp.shape, F32),
        ],
        scratch_shapes=[
            pltpu.VMEM((tm, ATTN_WIDTH), F32), pltpu.VMEM((tm, ATTN_WIDTH), F32),
            pltpu.VMEM((tm, 2 * KV_WIDTH), F32), pltpu.VMEM((tm, 2 * KV_WIDTH), F32),
            pltpu.VMEM((tm, POOL_WIDTH), F32), pltpu.VMEM((tm, POOL_WIDTH), F32),
            pltpu.VMEM((tm, MIX_WIDTH), BF16), pltpu.VMEM((tm, MIX_WIDTH), BF16),
            pltpu.VMEM((tm, D_MODEL), F32),
            pltpu.VMEM((KV_PADS * N_KV_HEADS, WINDOW, LANES), BF16),
            pltpu.VMEM((POOL_PREV_ROWS, POOL_WIDTH), F32),
        ],
        input_output_aliases={10: 1, 11: 2, 12: 3},
        compiler_params=_params("arbitrary"),
        name="mixer_prompt",
    )(sinks, x, x, gpre, gpost, w_in_b, w_out_b, bias, wpool, pscale, kp, vp, pp)


def _inproj_sample_kernel(x_ref, g_ref, w_ref, proj_ref, wb_ref, h_ref):
    j = pl.program_id(0)

    @pl.when(j == 0)
    def _():
        h_ref[...] = _rms(x_ref[...], g_ref[...]).astype(BF16)

    wb = w_ref[...].astype(BF16)
    wb_ref[...] = wb
    p = jnp.dot(h_ref[...], wb, preferred_element_type=F32)
    proj_ref[...] = p * jnp.where(j < ATTN_WIDTH // WEIGHT_BLOCK, Q_SCALE, 1.0)


def _inproj_sample(l, x, g, w):
    m = x.shape[0]
    tn = WEIGHT_BLOCK
    return pl.pallas_call(
        _inproj_sample_kernel,
        grid=(IN_WIDTH // tn,),
        in_specs=[
            pl.BlockSpec((m, D_MODEL), lambda j: (0, 0)),
            _layer_vec(l, D_MODEL),
            pl.BlockSpec((None, D_MODEL, tn), lambda j: (l, 0, j)),
        ],
        out_specs=[
            pl.BlockSpec((m, tn), lambda j: (0, j)),
            pl.BlockSpec((D_MODEL, tn), lambda j: (0, j)),
        ],
        out_shape=[
            jax.ShapeDtypeStruct((m, IN_WIDTH), F32),
            jax.ShapeDtypeStruct((D_MODEL, IN_WIDTH), BF16),
        ],
        scratch_shapes=[pltpu.VMEM((m, D_MODEL), BF16)],
        compiler_params=_params("arbitrary"),
        name="inproj_sample",
    )(x, g, w)


def _shift_in(old, new, n_new):
    r = old.shape[0]
    rolled = pltpu.roll(old, r - n_new, axis=0)
    tail = pltpu.roll(new, SUBLANES - n_new, axis=0)
    row = lax.broadcasted_iota(jnp.int32, (SUBLANES, old.shape[1]), 0)
    last = jnp.where(row < SUBLANES - n_new, rolled[r - SUBLANES:], tail)
    return jnp.concatenate([rolled[:r - SUBLANES], last], axis=0)


def _mixer_sample_kernel(l, n_new, sinks_ref, proj_ref, ck_ref, cv_ref, st_ref, bias_ref, wpool_ref, pscale_ref,
                         ks_any, vs_any, ps_any, out_ref, ko_ref, vo_ref, po_ref):
    del ks_any, vs_any, ps_any
    rows = SAMPLE_ROWS
    pad = jnp.zeros((WINDOW - rows, KV_WIDTH), F32)
    pos = PAST_LEN + lax.broadcasted_iota(jnp.int32, (rows, 1), 0)
    seqs = range(SAMPLE_GROUP)
    k_new = [proj_ref[g, :, ATTN_WIDTH:ATTN_WIDTH + KV_WIDTH] for g in seqs]
    v_new = [proj_ref[g, :, ATTN_WIDTH + KV_WIDTH:ATTN_WIDTH + 2 * KV_WIDTH] for g in seqs]
    kk = [jnp.concatenate([ck_ref[g], k_new[g], pad], axis=0) for g in seqs]
    vv = [jnp.concatenate([cv_ref[g], v_new[g], pad], axis=0) for g in seqs]
    items = [(kh, functools.partial(lambda g, c: proj_ref[g, :, c:c + LANES], g), _kv_pads(kk[g], vv[g], kh),
              lambda i: bias_ref[i], out_ref.at[g]) for kh in range(N_KV_HEADS) for g in seqs]
    _attention(items, rows, lambda hd: sinks_ref[l, hd])
    for g in seqs:
        u_prev, u_cur = st_ref[g], proj_ref[g, :, ATTN_WIDTH + 2 * KV_WIDTH:]
        _pool(rows, u_prev, u_cur, pos, wpool_ref, pscale_ref, out_ref.at[g])
        ko_ref[g] = _shift_in(ck_ref[g], k_new[g], n_new)
        vo_ref[g] = _shift_in(cv_ref[g], v_new[g], n_new)
        po_ref[g] = _last_rows(_shift_in(u_prev, u_cur, n_new), POOL_STATE)


def _mixer_sample(l, proj, cache_k, cache_v, state, sinks, bias, wpool, pscale, ks, vs, ps, n_new):
    n_seq = cache_k.shape[1]
    rows = SAMPLE_ROWS
    grp = SAMPLE_GROUP
    assert n_seq % grp == 0
    per_seq = lambda r, width: pl.BlockSpec((None, grp, r, width), lambda n: (l, n, 0, 0))
    seqs = lambda width: pl.BlockSpec((grp, rows, width), lambda n: (n, 0, 0))
    return pl.pallas_call(
        functools.partial(_mixer_sample_kernel, l, n_new),
        grid=(n_seq // grp,),
        in_specs=[
            pl.BlockSpec(memory_space=pltpu.SMEM),
            seqs(IN_WIDTH),
            per_seq(WINDOW, KV_WIDTH),
            per_seq(WINDOW, KV_WIDTH),
            per_seq(POOL_PREV_ROWS, POOL_WIDTH),
            pl.BlockSpec(bias.shape, lambda n: (0, 0, 0)),
            pl.BlockSpec((None,) + wpool.shape[1:], lambda n: (l, 0, 0, 0)),
            _layer_vec(l, POOL_WIDTH),
            pl.BlockSpec(memory_space=pl.ANY),
            pl.BlockSpec(memory_space=pl.ANY),
            pl.BlockSpec(memory_space=pl.ANY),
        ],
        out_specs=[
            seqs(MIX_WIDTH),
            per_seq(WINDOW, KV_WIDTH),
            per_seq(WINDOW, KV_WIDTH),
            per_seq(POOL_STATE, POOL_WIDTH),
        ],
        out_shape=[
            jax.ShapeDtypeStruct((n_seq, rows, MIX_WIDTH), F32),
            jax.ShapeDtypeStruct(ks.shape, F32),
            jax.ShapeDtypeStruct(vs.shape, F32),
            jax.ShapeDtypeStruct(ps.shape, F32),
        ],
        input_output_aliases={8: 1, 9: 2, 10: 3},
        compiler_params=_params("arbitrary"),
        name="mixer_sample",
    )(sinks, proj.reshape(n_seq, rows, IN_WIDTH), cache_k, cache_v, state, bias, wpool, pscale, ks, vs, ps)


def _outproj_sample_kernel(mix_ref, x_ref, g_ref, w_ref, o_ref, wb_ref):
    k = pl.program_id(0)

    @pl.when(k == 0)
    def _():
        o_ref[...] = jnp.zeros_like(o_ref)

    wb = w_ref[...].astype(BF16)
    wb_ref[...] = wb
    o_ref[...] += jnp.dot(mix_ref[...].astype(BF16), wb, preferred_element_type=F32)

    @pl.when(k == pl.num_programs(0) - 1)
    def _():
        o_ref[...] = x_ref[...] + _rms(o_ref[...], g_ref[...])


def _outproj_sample(l, mix, x, g, w):
    m = x.shape[0]
    tk = WEIGHT_BLOCK
    return pl.pallas_call(
        _outproj_sample_kernel,
        grid=(MIX_WIDTH // tk,),
        in_specs=[
            pl.BlockSpec((m, tk), lambda k: (0, k)),
            pl.BlockSpec((m, D_MODEL), lambda k: (0, 0)),
            _layer_vec(l, D_MODEL),
            pl.BlockSpec((None, tk, D_MODEL), lambda k: (l, k, 0)),
        ],
        out_specs=[
            pl.BlockSpec((m, D_MODEL), lambda k: (0, 0)),
            pl.BlockSpec((tk, D_MODEL), lambda k: (k, 0)),
        ],
        out_shape=[
            jax.ShapeDtypeStruct((m, D_MODEL), F32),
            jax.ShapeDtypeStruct((MIX_WIDTH, D_MODEL), BF16),
        ],
        compiler_params=_params("arbitrary"),
        name="outproj_sample",
    )(mix, x, g, w)


def _rms_chunked(src_ref, inv_ref, finish):
    n = src_ref.shape[0]

    def stats(c, carry):
        rows = pl.ds(pl.multiple_of(c * STAT_ROWS, STAT_ROWS), STAT_ROWS)
        y = src_ref[rows, :]
        inv = lax.rsqrt(jnp.mean(y * y, axis=-1, keepdims=True) + EPS)
        inv_ref[rows, :] = jnp.broadcast_to(inv, (STAT_ROWS, LANES))
        return carry

    def apply(c, carry):
        rows = pl.ds(pl.multiple_of(c * APPLY_ROWS, APPLY_ROWS), APPLY_ROWS)
        inv = jnp.tile(inv_ref[rows, :], (1, src_ref.shape[1] // LANES))
        finish(rows, src_ref[rows, :] * inv)
        return carry

    lax.fori_loop(0, n // STAT_ROWS, stats, 0, unroll=True)
    lax.fori_loop(0, n // APPLY_ROWS, apply, 0)


def _ffn_kernel(cast_weights, x_ref, gpre_ref, gpost_ref, wup_ref, wdn_ref, o_ref, *rest):
    f = pl.program_id(1)
    h_ref, inv_ref = rest[-2:]

    @pl.when(f == 0)
    def _():
        g = gpre_ref[...]

        def store_h(rows, xn):
            h_ref[rows, :] = (xn * g).astype(BF16)
            o_ref[rows, :] = jnp.zeros((APPLY_ROWS, D_MODEL), F32)

        _rms_chunked(x_ref, inv_ref, store_h)

    if cast_weights:
        wupb_ref, wdnb_ref = rest[:2]
        wupb_ref[...] = wup_ref[...].astype(BF16)
        wdnb_ref[...] = wdn_ref[...].astype(BF16)
        wup_ref, wdn_ref = wupb_ref, wdnb_ref
    a = jnp.dot(h_ref[...], wup_ref[...], preferred_element_type=F32)
    a = jnp.square(jnp.maximum(a, 0.0)).astype(BF16)
    o_ref[...] += jnp.dot(a, wdn_ref[...], preferred_element_type=F32)

    @pl.when(f == pl.num_programs(1) - 1)
    def _():
        g = gpost_ref[...]

        def store_out(rows, yn):
            o_ref[rows, :] = x_ref[rows, :] + yn * g

        _rms_chunked(o_ref, inv_ref, store_out)


def _ffn(l, x, gpre, gpost, wup, wdn, tm, tf, cast_weights):
    m = x.shape[0]
    if cast_weights:
        assert m == tm
        w_specs = [pl.BlockSpec((None, D_MODEL, tf), lambda i, f: (l, 0, f)),
                   pl.BlockSpec((None, tf, D_MODEL), lambda i, f: (l, f, 0))]
        extra_specs = [pl.BlockSpec((D_MODEL, tf), lambda i, f: (0, f)),
                       pl.BlockSpec((tf, D_MODEL), lambda i, f: (f, 0))]
        extra_shapes = [jax.ShapeDtypeStruct((D_MODEL, D_FF), BF16), jax.ShapeDtypeStruct((D_FF, D_MODEL), BF16)]
    else:
        w_specs = [pl.BlockSpec((D_MODEL, tf), lambda i, f: (0, f)),
                   pl.BlockSpec((tf, D_MODEL), lambda i, f: (f, 0))]
        extra_specs, extra_shapes = [], []
    return pl.pallas_call(
        functools.partial(_ffn_kernel, cast_weights),
        grid=(m // tm, D_FF // tf),
        in_specs=[
            pl.BlockSpec((tm, D_MODEL), lambda i, f: (i, 0)),
            _layer_vec(l, D_MODEL),
            _layer_vec(l, D_MODEL),
        ] + w_specs,
        out_specs=[pl.BlockSpec((tm, D_MODEL), lambda i, f: (i, 0))] + extra_specs,
        out_shape=[jax.ShapeDtypeStruct((m, D_MODEL), F32)] + extra_shapes,
        scratch_shapes=[pltpu.VMEM((tm, D_MODEL), BF16), pltpu.VMEM((tm, LANES), F32)],
        compiler_params=_params("arbitrary", "arbitrary"),
        name="ffn_sample" if cast_weights else "ffn_prompt",
    )(x, gpre, gpost, wup, wdn)


def _bias_tables():
    heads = np.arange(1, N_Q_HEADS + 1, dtype=np.float32)
    slopes = np.exp2(np.float32(-8.0) * heads / np.float32(N_Q_HEADS)).astype(np.float32)

    def table(rows, mask_prev):
        i = np.arange(rows)[:, None]
        j = np.arange(2 * WINDOW)[None, :]
        dist = i + WINDOW - j
        valid = (dist >= 0) & (dist < WINDOW)
        if mask_prev:
            valid = valid & (j >= WINDOW)
        out = np.empty((2 * N_KV_HEADS, 2 * rows, 2 * WINDOW), np.float32)
        for kh in range(N_KV_HEADS):
            for par in range(2):
                for half, head in enumerate((4 * kh + par, 4 * kh + 2 + par)):
                    bias = (-slopes[head]) * dist.astype(np.float32)
                    out[2 * kh + par, half * rows:(half + 1) * rows] = np.where(valid, bias, np.float32(NEG_BIG))
        return out

    prompt = np.stack([table(WINDOW, False), table(WINDOW, True)])
    sample = table(SAMPLE_ROWS, False)
    return jnp.asarray(prompt), jnp.asarray(sample)


def kernel(x_prompt, x_sample, cache_k, cache_v, state_pool, w_in, w_out, w_pool, pool_scale, attn_sinks,
           g_pre_mix, g_post_mix, g_pre_ffn, g_post_ffn, w_up, w_down):
    n_seq, seq, _ = x_prompt.shape
    dec_batch, dec_seq, _ = x_sample.shape
    assert seq % MIX_TM == 0 and (n_seq * seq) % FFN_TM == 0 and dec_seq <= SAMPLE_ROWS
    bias_prompt, bias_sample = _bias_tables()

    vec = lambda p: p.reshape(DEPTH, 1, p.shape[-1])
    gpm, gqm, gpf, gqf, pscale = vec(g_pre_mix), vec(g_post_mix), vec(g_pre_ffn), vec(g_post_ffn), vec(pool_scale)
    w_pool_b = w_pool.astype(BF16)

    xp = x_prompt.reshape(n_seq * seq, D_MODEL)
    xs = jnp.pad(x_sample, ((0, 0), (0, SAMPLE_ROWS - dec_seq), (0, 0))).reshape(dec_batch * SAMPLE_ROWS, D_MODEL)
    ck_all = cache_k.reshape(DEPTH, dec_batch, WINDOW, KV_WIDTH)
    cv_all = cache_v.reshape(DEPTH, dec_batch, WINDOW, KV_WIDTH)
    st_all = jnp.pad(state_pool, ((0, 0), (0, 0), (1, 0), (0, 0)))

    kp = jnp.zeros((DEPTH, n_seq, WINDOW, KV_WIDTH), F32)
    vp = jnp.zeros((DEPTH, n_seq, WINDOW, KV_WIDTH), F32)
    pp = jnp.zeros((DEPTH, n_seq, POOL_STATE, POOL_WIDTH), F32)
    ks = jnp.zeros((DEPTH, dec_batch, WINDOW, KV_WIDTH), F32)
    vs = jnp.zeros((DEPTH, dec_batch, WINDOW, KV_WIDTH), F32)
    ps = jnp.zeros((DEPTH, dec_batch, POOL_STATE, POOL_WIDTH), F32)

    for l in range(DEPTH):
        proj, w_in_b = _inproj_sample(l, xs, gpm, w_in)
        mix, ks, vs, ps = _mixer_sample(l, proj, ck_all, cv_all, st_all, attn_sinks, bias_sample, w_pool_b, pscale,
                                        ks, vs, ps, dec_seq)
        xs, w_out_b = _outproj_sample(l, mix.reshape(xs.shape[0], MIX_WIDTH), xs, gqm, w_out)
        xs, w_up_b, w_down_b = _ffn(l, xs, gpf, gqf, w_up, w_down, xs.shape[0], WEIGHT_BLOCK, True)

        xp, kp, vp, pp = _mixer_prompt(l, xp, attn_sinks, gpm, gqm, w_in_b, w_out_b, bias_prompt, w_pool_b, pscale,
                                       kp, vp, pp, seq)
        (xp,) = _ffn(l, xp, gpf, gqf, w_up_b, w_down_b, FFN_TM, FFN_TF, False)

    y_prompt = xp.reshape(n_seq, seq, D_MODEL)
    y_sample = xs.reshape(dec_batch, SAMPLE_ROWS, D_MODEL)[:, :dec_seq]
    kv_shape = (WINDOW, N_KV_HEADS, HEAD_DIM)
    return (y_prompt, y_sample, kp.reshape((DEPTH, n_seq) + kv_shape), vp.reshape((DEPTH, n_seq) + kv_shape), pp,
            ks.reshape((DEPTH, dec_batch) + kv_shape), vs.reshape((DEPTH, dec_batch) + kv_shape), ps)
```

```python
import functools

import jax
import jax.numpy as jnp
import numpy as np
from jax import lax
from jax.experimental import pallas as pl
from jax.experimental.pallas import tpu as pltpu

D_MODEL = 2048
DEPTH = 4
PAST_LEN = 16384
HEAD_DIM = 64
N_Q_HEADS = 16
N_KV_HEADS = 4
ATTN_WIDTH = N_Q_HEADS * HEAD_DIM
KV_WIDTH = N_KV_HEADS * HEAD_DIM
WINDOW = 128
POOL_WINDOWS = (2, 4, 8, 16)
POOL_GROUP_WIDTH = 256
POOL_WIDTH = len(POOL_WINDOWS) * POOL_GROUP_WIDTH
POOL_STATE = max(POOL_WINDOWS) - 1
POOL_PREV_ROWS = POOL_STATE + 1
MIX_WIDTH = ATTN_WIDTH + POOL_WIDTH
IN_WIDTH = ATTN_WIDTH + 2 * KV_WIDTH + POOL_WIDTH
D_FF = 4 * D_MODEL
EPS = 1e-6
Q_SCALE = HEAD_DIM ** -0.5

LANES = 128
SUBLANES = 8
NEG_BIG = -1e30
VMEM_LIMIT_BYTES = 56 * 1024 * 1024
SAMPLE_ROWS = SUBLANES

FFN_TM = 512
FFN_TF = 1024
STAT_ROWS = 256
APPLY_ROWS = 64
SAMPLE_GROUP = 4
WEIGHT_BLOCK = 512
MIX_TM = 256
MXU_CHUNK = 256
MIX_LAG = 2
KV_PADS = 4

BF16 = jnp.bfloat16
F32 = jnp.float32


def _rms(x, g):
    ms = jnp.mean(x * x, axis=-1, keepdims=True)
    return (x * lax.rsqrt(ms + EPS)) * g


def _params(*semantics):
    return pltpu.CompilerParams(dimension_semantics=semantics, vmem_limit_bytes=VMEM_LIMIT_BYTES)


def _layer_vec(l, width):
    return pl.BlockSpec((None, 1, width), lambda *_: (l, 0, 0))


def _resident(shape, index_map):
    return pl.BlockSpec(shape, index_map, pipeline_mode=pl.Buffered(1))


def _last_rows(x, n):
    tail = x[x.shape[0] - 2 * SUBLANES:]
    return pltpu.roll(tail, n, axis=0)[:n]


def _half_padded(x2, head_in_high_half):
    lane = lax.broadcasted_iota(jnp.int32, x2.shape, 1)
    swapped = pltpu.roll(x2, HEAD_DIM, axis=1)
    in_lo, in_hi = (swapped, x2) if head_in_high_half else (x2, swapped)
    lo = jnp.where(lane < HEAD_DIM, in_lo, 0.0).astype(BF16)
    hi = jnp.where(lane >= HEAD_DIM, in_hi, 0.0).astype(BF16)
    return lo, hi


def _kv_pads(k, v, kh):
    col = (kh // 2) * LANES
    high = kh % 2 == 1
    return _half_padded(k[:, col:col + LANES], high) + _half_padded(v[:, col:col + LANES], high)


def _attention(items, rows, sink_at, between=lambda: None):
    keys = 2 * WINDOW
    row = lax.broadcasted_iota(jnp.int32, (2 * rows, 1), 0)
    lane = lax.broadcasted_iota(jnp.int32, (2 * rows, LANES), 1)

    scores = []
    for kh, q_at, (k_lo, k_hi, _, _), bias_at, _ in items:
        c0 = 2 * kh * LANES
        qq = jnp.concatenate([q_at(c0), q_at(c0 + LANES)], axis=0).astype(BF16)
        s = lax.dot_general(qq, jnp.concatenate([k_lo, k_hi], axis=0), (((1,), (1,)), ((), ())),
                            preferred_element_type=F32)
        scores.append([s[:, par * keys:(par + 1) * keys] + bias_at(2 * kh + par) for par in range(2)])
        between()

    sinks = [[jnp.where(row < rows, sink_at(4 * kh + par), sink_at(4 * kh + 2 + par)) for par in range(2)]
             for kh, *_ in items]
    maxes = []
    for pair, sks in zip(scores, sinks):
        maxes.append([jnp.maximum(jnp.max(s, axis=-1, keepdims=True), sk) for s, sk in zip(pair, sks)])
        between()
    probs, recips = [], []
    for pair, sks, ms in zip(scores, sinks, maxes):
        probs.append([jnp.exp(s - m) for s, m in zip(pair, ms)])
        recips.append([1.0 / (jnp.sum(p, axis=-1, keepdims=True) + jnp.exp(sk - m))
                       for p, sk, m in zip(probs[-1], sks, ms)])
        between()

    for (kh, _, (_, _, v_lo, v_hi), _, out_ref), pair, rcp in zip(items, probs, recips):
        c0 = 2 * kh * LANES
        p_cat = jnp.concatenate([p.astype(BF16) for p in pair], axis=1)
        o = jnp.dot(p_cat, jnp.concatenate([v_lo, v_hi], axis=0), preferred_element_type=F32)
        o = o * jnp.where(lane < HEAD_DIM, rcp[0], rcp[1])
        out_ref[:, c0:c0 + LANES] = o[:rows].astype(out_ref.dtype)
        out_ref[:, c0 + LANES:c0 + 2 * LANES] = o[rows:].astype(out_ref.dtype)
        between()


def _pool(rows, u_prev, u_cur, pos, wpool_ref, pscale_ref, out_ref):
    ext = jnp.concatenate([u_prev, u_cur], axis=0)
    for gi, w in enumerate(POOL_WINDOWS):
        lo = gi * POOL_GROUP_WIDTH
        e = ext[:, lo:lo + POOL_GROUP_WIDTH]
        s, d = e, 1
        while d < w:
            s = s + pltpu.roll(s, d, axis=0)
            d *= 2
        cnt = jnp.minimum(pos + 1, w).astype(F32)
        z = s[POOL_PREV_ROWS:] / cnt - e[POOL_PREV_ROWS:]
        zz = jnp.dot(z.astype(BF16), wpool_ref[gi], preferred_element_type=F32)
        zz = zz * pscale_ref[:, lo:lo + POOL_GROUP_WIDTH]
        out_ref[:, ATTN_WIDTH + lo:ATTN_WIDTH + lo + POOL_GROUP_WIDTH] = zz.astype(out_ref.dtype)


def _mixer_prompt_kernel(l, tiles_per_seq, n_tiles, sinks_ref, x_ref, xo_ref, gpre_ref, gpost_ref, win_ref, wout_ref,
                         bias_ref, wpool_ref, pscale_ref, kp_any, vp_any, pp_any,
                         out_ref, kp_ref, vp_ref, pp_ref, *scratch):
    del kp_any, vp_any, pp_any
    q_s, kv_s, u_s, mix_s = scratch[0:2], scratch[2:4], scratch[4:6], scratch[6:8]
    y_s, pads_s, utail_s = scratch[8:]
    s = pl.program_id(0)
    blocks = MIX_TM // WINDOW
    block_rows = [pl.ds(j * WINDOW, WINDOW) for j in range(blocks)]

    @pl.when(s == 0)
    def _():
        pads_s[...] = jnp.zeros_like(pads_s)
        utail_s[...] = jnp.zeros_like(utail_s)

    def step(new, project_in, attend, project_out):
        old = 1 - new
        if project_in:
            h = _rms(x_ref[...], gpre_ref[...]).astype(BF16)

        def in_projection_chunk(c):
            r = jnp.dot(h, win_ref[:, c:c + MXU_CHUNK], preferred_element_type=F32)
            if c < ATTN_WIDTH:
                q_s[new][:, c:c + MXU_CHUNK] = r * Q_SCALE
            elif c < ATTN_WIDTH + 2 * KV_WIDTH:
                kv_s[new][:, c - ATTN_WIDTH:c - ATTN_WIDTH + MXU_CHUNK] = r
            else:
                c -= ATTN_WIDTH + 2 * KV_WIDTH
                u_s[new][:, c:c + MXU_CHUNK] = r

        def out_projection_chunk(c):
            y_s[:, c:c + MXU_CHUNK] = jnp.dot(mix_s[new][...], wout_ref[:, c:c + MXU_CHUNK],
                                              preferred_element_type=F32)

        def out_projection_finish():
            out_ref[...] = xo_ref[...] + _rms(y_s[...], gpost_ref[...])

        pending = []
        if project_out:
            pending += [functools.partial(out_projection_chunk, c) for c in range(0, D_MODEL, MXU_CHUNK)]
            pending += [out_projection_finish]
        if project_in:
            pending += [functools.partial(in_projection_chunk, c) for c in range(0, IN_WIDTH, MXU_CHUNK)]
        n_pending = len(pending)
        n_slots = 4 * N_KV_HEADS * blocks
        slots = [0]

        def between():
            slots[0] += 1
            while n_pending - len(pending) < slots[0] * n_pending // n_slots:
                pending.pop(0)()

        if not attend:
            while pending:
                pending.pop(0)()
            if project_in:
                state_outputs(new)
            return

        tile = (s - 1) % tiles_per_seq
        seq_start = tile == 0
        first = seq_start.astype(jnp.int32)
        prev_pads = [pads_s[i] for i in range(KV_PADS * N_KV_HEADS)]
        items = []
        for kh in range(N_KV_HEADS):
            pads = prev_pads[KV_PADS * kh:KV_PADS * (kh + 1)]
            for j, rows in enumerate(block_rows):
                cur = _kv_pads(kv_s[old][rows, :KV_WIDTH], kv_s[old][rows, KV_WIDTH:], kh)
                both = [jnp.concatenate([p, c], axis=0) for p, c in zip(pads, cur)]
                bias_at = (lambda i: bias_ref[first, i]) if j == 0 else (lambda i: bias_ref[0, i])
                items.append((kh, functools.partial(lambda rows, c: q_s[old][rows, c:c + LANES], rows), both,
                              bias_at, mix_s[old].at[rows]))
                pads = cur
            for i, p in enumerate(pads):
                pads_s[KV_PADS * kh + i] = p
        _attention(items, WINDOW, lambda hd: sinks_ref[l, hd], between)
        assert not pending

        u_prev = jnp.where(seq_start, 0.0, utail_s[...])
        for j, rows in enumerate(block_rows):
            u_cur = u_s[old][rows, :]
            pos = (tile * blocks + j) * WINDOW + lax.broadcasted_iota(jnp.int32, (WINDOW, 1), 0)
            _pool(WINDOW, u_prev, u_cur, pos, wpool_ref, pscale_ref, mix_s[old].at[rows])
            u_prev = u_cur[WINDOW - POOL_PREV_ROWS:]
        utail_s[...] = u_prev
        if project_in:
            state_outputs(new)

    def state_outputs(new):
        @pl.when(s % tiles_per_seq == tiles_per_seq - 1)
        def _():
            kp_ref[...] = kv_s[new][MIX_TM - WINDOW:, :KV_WIDTH]
            vp_ref[...] = kv_s[new][MIX_TM - WINDOW:, KV_WIDTH:]
            pp_ref[...] = _last_rows(u_s[new][MIX_TM - 2 * SUBLANES:, :], POOL_STATE)

    pl.when(s == 0)(functools.partial(step, 0, True, False, False))
    pl.when(s == 1)(functools.partial(step, 1, True, True, False))
    for new in range(2):
        pl.when((s >= MIX_LAG) & (s < n_tiles) & (s % 2 == new))(functools.partial(step, new, True, True, True))
    pl.when(s == n_tiles)(functools.partial(step, 0, False, True, True))
    pl.when(s == n_tiles + 1)(functools.partial(step, 1, False, False, True))


def _mixer_prompt(l, x, sinks, gpre, gpost, w_in_b, w_out_b, bias, wpool, pscale, kp, vp, pp, seq):
    m = x.shape[0]
    tm = MIX_TM
    tiles_per_seq = seq // tm
    n_tiles = m // tm
    assert n_tiles % 2 == 0 and n_tiles > MIX_LAG
    cur = lambda s: (jnp.minimum(s, n_tiles - 1), 0)
    lagging = lambda s: (jnp.maximum(s - MIX_LAG, 0), 0)
    state = lambda rows, width: pl.BlockSpec((None, None, rows, width),
                                             lambda s: (l, jnp.minimum(s, n_tiles - 1) // tiles_per_seq, 0, 0))
    return pl.pallas_call(
        functools.partial(_mixer_prompt_kernel, l, tiles_per_seq, n_tiles),
        grid=(n_tiles + MIX_LAG,),
        in_specs=[
            pl.BlockSpec(memory_space=pltpu.SMEM),
            pl.BlockSpec((tm, D_MODEL), cur),
            pl.BlockSpec((tm, D_MODEL), lagging),
            _layer_vec(l, D_MODEL),
            _layer_vec(l, D_MODEL),
            _resident((D_MODEL, IN_WIDTH), lambda s: (0, 0)),
            _resident((MIX_WIDTH, D_MODEL), lambda s: (0, 0)),
            _resident(bias.shape, lambda s: (0, 0, 0, 0)),
            _resident((None,) + wpool.shape[1:], lambda s: (l, 0, 0, 0)),
            _layer_vec(l, POOL_WIDTH),
            pl.BlockSpec(memory_space=pl.ANY),
            pl.BlockSpec(memory_space=pl.ANY),
            pl.BlockSpec(memory_space=pl.ANY),
        ],
        out_specs=[
            pl.BlockSpec((tm, D_MODEL), lagging),
            state(WINDOW, KV_WIDTH),
            state(WINDOW, KV_WIDTH),
            state(POOL_STATE, POOL_WIDTH),
        ],
        out_shape=[
            jax.ShapeDtypeStruct((m, D_MODEL), F32),
            jax.ShapeDtypeStruct(kp.shape, F32),
            jax.ShapeDtypeStruct(vp.shape, F32),
            jax.ShapeDtypeStruct(pp.shape, F32),
        ],
        scratch_shapes=[
            pltpu.VMEM((tm, ATTN_WIDTH), F32), pltpu.VMEM((tm, ATTN_WIDTH), F32),
            pltpu.VMEM((tm, 2 * KV_WIDTH), F32), pltpu.VMEM((tm, 2 * KV_WIDTH), F32),
            pltpu.VMEM((tm, POOL_WIDTH), F32), pltpu.VMEM((tm, POOL_WIDTH), F32),
            pltpu.VMEM((tm, MIX_WIDTH), BF16), pltpu.VMEM((tm, MIX_WIDTH), BF16),
            pltpu.VMEM((tm, D_MODEL), F32),
            pltpu.VMEM((KV_PADS * N_KV_HEADS, WINDOW, LANES), BF16),
            pltpu.VMEM((POOL_PREV_ROWS, POOL_WIDTH), F32),
        ],
        input_output_aliases={10: 1, 11: 2, 12: 3},
        compiler_params=_params("arbitrary"),
        name="mixer_prompt",
    )(sinks, x, x, gpre, gpost, w_in_b, w_out_b, bias, wpool, pscale, kp, vp, pp)


def _inproj_sample_kernel(x_ref, g_ref, w_ref, proj_ref, wb_ref, h_ref):
    j = pl.program_id(0)

    @pl.when(j == 0)
    def _():
        h_ref[...] = _rms(x_ref[...], g_ref[...]).astype(BF16)

    wb = w_ref[...].astype(BF16)
    wb_ref[...] = wb
    p = jnp.dot(h_ref[...], wb, preferred_element_type=F32)
    proj_ref[...] = p * jnp.where(j < ATTN_WIDTH // WEIGHT_BLOCK, Q_SCALE, 1.0)


def _inproj_sample(l, x, g, w):
    m = x.shape[0]
    tn = WEIGHT_BLOCK
    return pl.pallas_call(
        _inproj_sample_kernel,
        grid=(IN_WIDTH // tn,),
        in_specs=[
            pl.BlockSpec((m, D_MODEL), lambda j: (0, 0)),
            _layer_vec(l, D_MODEL),
            pl.BlockSpec((None, D_MODEL, tn), lambda j: (l, 0, j)),
        ],
        out_specs=[
            pl.BlockSpec((m, tn), lambda j: (0, j)),
            pl.BlockSpec((D_MODEL, tn), lambda j: (0, j)),
        ],
        out_shape=[
            jax.ShapeDtypeStruct((m, IN_WIDTH), F32),
            jax.ShapeDtypeStruct((D_MODEL, IN_WIDTH), BF16),
        ],
        scratch_shapes=[pltpu.VMEM((m, D_MODEL), BF16)],
        compiler_params=_params("arbitrary"),
        name="inproj_sample",
    )(x, g, w)


def _shift_in(old, new, n_new):
    r = old.shape[0]
    rolled = pltpu.roll(old, r - n_new, axis=0)
    tail = pltpu.roll(new, SUBLANES - n_new, axis=0)
    row = lax.broadcasted_iota(jnp.int32, (SUBLANES, old.shape[1]), 0)
    last = jnp.where(row < SUBLANES - n_new, rolled[r - SUBLANES:], tail)
    return jnp.concatenate([rolled[:r - SUBLANES], last], axis=0)


def _mixer_sample_kernel(l, n_new, sinks_ref, proj_ref, ck_ref, cv_ref, st_ref, bias_ref, wpool_ref, pscale_ref,
                         ks_any, vs_any, ps_any, out_ref, ko_ref, vo_ref, po_ref):
    del ks_any, vs_any, ps_any
    rows = SAMPLE_ROWS
    pad = jnp.zeros((WINDOW - rows, KV_WIDTH), F32)
    pos = PAST_LEN + lax.broadcasted_iota(jnp.int32, (rows, 1), 0)
    seqs = range(SAMPLE_GROUP)
    k_new = [proj_ref[g, :, ATTN_WIDTH:ATTN_WIDTH + KV_WIDTH] for g in seqs]
    v_new = [proj_ref[g, :, ATTN_WIDTH + KV_WIDTH:ATTN_WIDTH + 2 * KV_WIDTH] for g in seqs]
    kk = [jnp.concatenate([ck_ref[g], k_new[g], pad], axis=0) for g in seqs]
    vv = [jnp.concatenate([cv_ref[g], v_new[g], pad], axis=0) for g in seqs]
    items = [(kh, functools.partial(lambda g, c: proj_ref[g, :, c:c + LANES], g), _kv_pads(kk[g], vv[g], kh),
              lambda i: bias_ref[i], out_ref.at[g]) for kh in range(N_KV_HEADS) for g in seqs]
    _attention(items, rows, lambda hd: sinks_ref[l, hd])
    for g in seqs:
        u_prev, u_cur = st_ref[g], proj_ref[g, :, ATTN_WIDTH + 2 * KV_WIDTH:]
        _pool(rows, u_prev, u_cur, pos, wpool_ref, pscale_ref, out_ref.at[g])
        ko_ref[g] = _shift_in(ck_ref[g], k_new[g], n_new)
        vo_ref[g] = _shift_in(cv_ref[g], v_new[g], n_new)
        po_ref[g] = _last_rows(_shift_in(u_prev, u_cur, n_new), POOL_STATE)


def _mixer_sample(l, proj, cache_k, cache_v, state, sinks, bias, wpool, pscale, ks, vs, ps, n_new):
    n_seq = cache_k.shape[1]
    rows = SAMPLE_ROWS
    grp = SAMPLE_GROUP
    assert n_seq % grp == 0
    per_seq = lambda r, width: pl.BlockSpec((None, grp, r, width), lambda n: (l, n, 0, 0))
    seqs = lambda width: pl.BlockSpec((grp, rows, width), lambda n: (n, 0, 0))
    return pl.pallas_call(
        functools.partial(_mixer_sample_kernel, l, n_new),
        grid=(n_seq // grp,),
        in_specs=[
            pl.BlockSpec(memory_space=pltpu.SMEM),
            seqs(IN_WIDTH),
            per_seq(WINDOW, KV_WIDTH),
            per_seq(WINDOW, KV_WIDTH),
            per_seq(POOL_PREV_ROWS, POOL_WIDTH),
            pl.BlockSpec(bias.shape, lambda n: (0, 0, 0)),
            pl.BlockSpec((None,) + wpool.shape[1:], lambda n: (l, 0, 0, 0)),
            _layer_vec(l, POOL_WIDTH),
            pl.BlockSpec(memory_space=pl.ANY),
            pl.BlockSpec(memory_space=pl.ANY),
            pl.BlockSpec(memory_space=pl.ANY),
        ],
        out_specs=[
            seqs(MIX_WIDTH),
            per_seq(WINDOW, KV_WIDTH),
            per_seq(WINDOW, KV_WIDTH),
            per_seq(POOL_STATE, POOL_WIDTH),
        ],
        out_shape=[
            jax.ShapeDtypeStruct((n_seq, rows, MIX_WIDTH), F32),
            jax.ShapeDtypeStruct(ks.shape, F32),
            jax.ShapeDtypeStruct(vs.shape, F32),
            jax.ShapeDtypeStruct(ps.shape, F32),
        ],
        input_output_aliases={8: 1, 9: 2, 10: 3},
        compiler_params=_params("arbitrary"),
        name="mixer_sample",
    )(sinks, proj.reshape(n_seq, rows, IN_WIDTH), cache_k, cache_v, state, bias, wpool, pscale, ks, vs, ps)


def _outproj_sample_kernel(mix_ref, x_ref, g_ref, w_ref, o_ref, wb_ref):
    k = pl.program_id(0)

    @pl.when(k == 0)
    def _():
        o_ref[...] = jnp.zeros_like(o_ref)

    wb = w_ref[...].astype(BF16)
    wb_ref[...] = wb
    o_ref[...] += jnp.dot(mix_ref[...].astype(BF16), wb, preferred_element_type=F32)

    @pl.when(k == pl.num_programs(0) - 1)
    def _():
        o_ref[...] = x_ref[...] + _rms(o_ref[...], g_ref[...])


def _outproj_sample(l, mix, x, g, w):
    m = x.shape[0]
    tk = WEIGHT_BLOCK
    return pl.pallas_call(
        _outproj_sample_kernel,
        grid=(MIX_WIDTH // tk,),
        in_specs=[
            pl.BlockSpec((m, tk), lambda k: (0, k)),
            pl.BlockSpec((m, D_MODEL), lambda k: (0, 0)),
            _layer_vec(l, D_MODEL),
            pl.BlockSpec((None, tk, D_MODEL), lambda k: (l, k, 0)),
        ],
        out_specs=[
            pl.BlockSpec((m, D_MODEL), lambda k: (0, 0)),
            pl.BlockSpec((tk, D_MODEL), lambda k: (k, 0)),
        ],
        out_shape=[
            jax.ShapeDtypeStruct((m, D_MODEL), F32),
            jax.ShapeDtypeStruct((MIX_WIDTH, D_MODEL), BF16),
        ],
        compiler_params=_params("arbitrary"),
        name="outproj_sample",
    )(mix, x, g, w)


def _rms_chunked(src_ref, inv_ref, finish):
    n = src_ref.shape[0]

    def stats(c, carry):
        rows = pl.ds(pl.multiple_of(c * STAT_ROWS, STAT_ROWS), STAT_ROWS)
        y = src_ref[rows, :]
        inv = lax.rsqrt(jnp.mean(y * y, axis=-1, keepdims=True) + EPS)
        inv_ref[rows, :] = jnp.broadcast_to(inv, (STAT_ROWS, LANES))
        return carry

    def apply(c, carry):
        rows = pl.ds(pl.multiple_of(c * APPLY_ROWS, APPLY_ROWS), APPLY_ROWS)
        inv = jnp.tile(inv_ref[rows, :], (1, src_ref.shape[1] // LANES))
        finish(rows, src_ref[rows, :] * inv)
        return carry

    lax.fori_loop(0, n // STAT_ROWS, stats, 0, unroll=True)
    lax.fori_loop(0, n // APPLY_ROWS, apply, 0)


def _ffn_kernel(cast_weights, x_ref, gpre_ref, gpost_ref, wup_ref, wdn_ref, o_ref, *rest):
    f = pl.program_id(1)
    h_ref, inv_ref = rest[-2:]

    @pl.when(f == 0)
    def _():
        g = gpre_ref[...]

        def store_h(rows, xn):
            h_ref[rows, :] = (xn * g).astype(BF16)
            o_ref[rows, :] = jnp.zeros((APPLY_ROWS, D_MODEL), F32)

        _rms_chunked(x_ref, inv_ref, store_h)

    if cast_weights:
        wupb_ref, wdnb_ref = rest[:2]
        wupb_ref[...] = wup_ref[...].astype(BF16)
        wdnb_ref[...] = wdn_ref[...].astype(BF16)
        wup_ref, wdn_ref = wupb_ref, wdnb_ref
    a = jnp.dot(h_ref[...], wup_ref[...], preferred_element_type=F32)
    a = jnp.square(jnp.maximum(a, 0.0)).astype(BF16)
    o_ref[...] += jnp.dot(a, wdn_ref[...], preferred_element_type=F32)

    @pl.when(f == pl.num_programs(1) - 1)
    def _():
        g = gpost_ref[...]

        def store_out(rows, yn):
            o_ref[rows, :] = x_ref[rows, :] + yn * g

        _rms_chunked(o_ref, inv_ref, store_out)


def _ffn(l, x, gpre, gpost, wup, wdn, tm, tf, cast_weights):
    m = x.shape[0]
    if cast_weights:
        assert m == tm
        w_specs = [pl.BlockSpec((None, D_MODEL, tf), lambda i, f: (l, 0, f)),
                   pl.BlockSpec((None, tf, D_MODEL), lambda i, f: (l, f, 0))]
        per_slab = FFN_TF // tf
        extra_specs = [pl.BlockSpec((None, D_MODEL, tf), lambda i, f: (f // per_slab, 0, f % per_slab)),
                       pl.BlockSpec((tf, D_MODEL), lambda i, f: (f, 0))]
        extra_shapes = [jax.ShapeDtypeStruct((D_FF // FFN_TF, D_MODEL, FFN_TF), BF16),
                        jax.ShapeDtypeStruct((D_FF, D_MODEL), BF16)]
    else:
        w_specs = [pl.BlockSpec((None, D_MODEL, tf), lambda i, f: (f, 0, 0)),
                   pl.BlockSpec((tf, D_MODEL), lambda i, f: (f, 0))]
        extra_specs, extra_shapes = [], []
    return pl.pallas_call(
        functools.partial(_ffn_kernel, cast_weights),
        grid=(m // tm, D_FF // tf),
        in_specs=[
            pl.BlockSpec((tm, D_MODEL), lambda i, f: (i, 0)),
            _layer_vec(l, D_MODEL),
            _layer_vec(l, D_MODEL),
        ] + w_specs,
        out_specs=[pl.BlockSpec((tm, D_MODEL), lambda i, f: (i, 0))] + extra_specs,
        out_shape=[jax.ShapeDtypeStruct((m, D_MODEL), F32)] + extra_shapes,
        scratch_shapes=[pltpu.VMEM((tm, D_MODEL), BF16), pltpu.VMEM((tm, LANES), F32)],
        compiler_params=_params("arbitrary", "arbitrary"),
        name="ffn_sample" if cast_weights else "ffn_prompt",
    )(x, gpre, gpost, wup, wdn)


def _bias_tables():
    heads = np.arange(1, N_Q_HEADS + 1, dtype=np.float32)
    slopes = np.exp2(np.float32(-8.0) * heads / np.float32(N_Q_HEADS)).astype(np.float32)

    def table(rows, mask_prev):
        i = np.arange(rows)[:, None]
        j = np.arange(2 * WINDOW)[None, :]
        dist = i + WINDOW - j
        valid = (dist >= 0) & (dist < WINDOW)
        if mask_prev:
            valid = valid & (j >= WINDOW)
        out = np.empty((2 * N_KV_HEADS, 2 * rows, 2 * WINDOW), np.float32)
        for kh in range(N_KV_HEADS):
            for par in range(2):
                for half, head in enumerate((4 * kh + par, 4 * kh + 2 + par)):
                    bias = (-slopes[head]) * dist.astype(np.float32)
                    out[2 * kh + par, half * rows:(half + 1) * rows] = np.where(valid, bias, np.float32(NEG_BIG))
        return out

    prompt = np.stack([table(WINDOW, False), table(WINDOW, True)])
    sample = table(SAMPLE_ROWS, False)
    return jnp.asarray(prompt), jnp.asarray(sample)


def kernel(x_prompt, x_sample, cache_k, cache_v, state_pool, w_in, w_out, w_pool, pool_scale, attn_sinks,
           g_pre_mix, g_post_mix, g_pre_ffn, g_post_ffn, w_up, w_down):
    n_seq, seq, _ = x_prompt.shape
    dec_batch, dec_seq, _ = x_sample.shape
    assert seq % MIX_TM == 0 and (n_seq * seq) % FFN_TM == 0 and dec_seq <= SAMPLE_ROWS
    bias_prompt, bias_sample = _bias_tables()

    vec = lambda p: p.reshape(DEPTH, 1, p.shape[-1])
    gpm, gqm, gpf, gqf, pscale = vec(g_pre_mix), vec(g_post_mix), vec(g_pre_ffn), vec(g_post_ffn), vec(pool_scale)
    w_pool_b = w_pool.astype(BF16)

    xp = x_prompt.reshape(n_seq * seq, D_MODEL)
    xs = jnp.pad(x_sample, ((0, 0), (0, SAMPLE_ROWS - dec_seq), (0, 0))).reshape(dec_batch * SAMPLE_ROWS, D_MODEL)
    ck_all = cache_k.reshape(DEPTH, dec_batch, WINDOW, KV_WIDTH)
    cv_all = cache_v.reshape(DEPTH, dec_batch, WINDOW, KV_WIDTH)
    st_all = jnp.pad(state_pool, ((0, 0), (0, 0), (1, 0), (0, 0)))

    kp = jnp.zeros((DEPTH, n_seq, WINDOW, KV_WIDTH), F32)
    vp = jnp.zeros((DEPTH, n_seq, WINDOW, KV_WIDTH), F32)
    pp = jnp.zeros((DEPTH, n_seq, POOL_STATE, POOL_WIDTH), F32)
    ks = jnp.zeros((DEPTH, dec_batch, WINDOW, KV_WIDTH), F32)
    vs = jnp.zeros((DEPTH, dec_batch, WINDOW, KV_WIDTH), F32)
    ps = jnp.zeros((DEPTH, dec_batch, POOL_STATE, POOL_WIDTH), F32)

    for l in range(DEPTH):
        proj, w_in_b = _inproj_sample(l, xs, gpm, w_in)
        mix, ks, vs, ps = _mixer_sample(l, proj, ck_all, cv_all, st_all, attn_sinks, bias_sample, w_pool_b, pscale,
                                        ks, vs, ps, dec_seq)
        xs, w_out_b = _outproj_sample(l, mix.reshape(xs.shape[0], MIX_WIDTH), xs, gqm, w_out)
        xs, w_up_b, w_down_b = _ffn(l, xs, gpf, gqf, w_up, w_down, xs.shape[0], WEIGHT_BLOCK, True)

        xp, kp, vp, pp = _mixer_prompt(l, xp, attn_sinks, gpm, gqm, w_in_b, w_out_b, bias_prompt, w_pool_b, pscale,
                                       kp, vp, pp, seq)
        (xp,) = _ffn(l, xp, gpf, gqf, w_up_b, w_down_b, FFN_TM, FFN_TF, False)

    y_prompt = xp.reshape(n_seq, seq, D_MODEL)
    y_sample = xs.reshape(dec_batch, SAMPLE_ROWS, D_MODEL)[:, :dec_seq]
    kv_shape = (WINDOW, N_KV_HEADS, HEAD_DIM)
    return (y_prompt, y_sample, kp.reshape((DEPTH, n_seq) + kv_shape), vp.reshape((DEPTH, n_seq) + kv_shape), pp,
            ks.reshape((DEPTH, dec_batch) + kv_shape), vs.reshape((DEPTH, dec_batch) + kv_shape), ps)
```

```python
import functools

import jax
import jax.numpy as jnp
import numpy as np
from jax import lax
from jax.experimental import pallas as pl
from jax.experimental.pallas import tpu as pltpu

D_MODEL = 2048
DEPTH = 4
PAST_LEN = 16384
HEAD_DIM = 64
N_Q_HEADS = 16
N_KV_HEADS = 4
ATTN_WIDTH = N_Q_HEADS * HEAD_DIM
KV_WIDTH = N_KV_HEADS * HEAD_DIM
WINDOW = 128
POOL_WINDOWS = (2, 4, 8, 16)
POOL_GROUP_WIDTH = 256
POOL_WIDTH = len(POOL_WINDOWS) * POOL_GROUP_WIDTH
POOL_STATE = max(POOL_WINDOWS) - 1
POOL_PREV_ROWS = POOL_STATE + 1
MIX_WIDTH = ATTN_WIDTH + POOL_WIDTH
IN_WIDTH = ATTN_WIDTH + 2 * KV_WIDTH + POOL_WIDTH
D_FF = 4 * D_MODEL
EPS = 1e-6
Q_SCALE = HEAD_DIM ** -0.5

LANES = 128
SUBLANES = 8
NEG_BIG = -1e30
VMEM_LIMIT_BYTES = 56 * 1024 * 1024
SAMPLE_ROWS = SUBLANES

FFN_TM = 512
FFN_TF = 1024
STAT_ROWS = 256
APPLY_ROWS = 64
SAMPLE_GROUP = 4
WEIGHT_BLOCK = 512
MIX_TM = 256
MXU_CHUNK = 256
MIX_LAG = 2
KV_PADS = 4

BF16 = jnp.bfloat16
F32 = jnp.float32


def _rms(x, g):
    ms = jnp.mean(x * x, axis=-1, keepdims=True)
    return (x * lax.rsqrt(ms + EPS)) * g


def _params(*semantics):
    return pltpu.CompilerParams(dimension_semantics=semantics, vmem_limit_bytes=VMEM_LIMIT_BYTES)


def _layer_vec(l, width):
    return pl.BlockSpec((None, 1, width), lambda *_: (l, 0, 0))


def _resident(shape, index_map):
    return pl.BlockSpec(shape, index_map, pipeline_mode=pl.Buffered(1))


def _last_rows(x, n):
    tail = x[x.shape[0] - 2 * SUBLANES:]
    return pltpu.roll(tail, n, axis=0)[:n]


def _half_padded(x2, head_in_high_half):
    lane = lax.broadcasted_iota(jnp.int32, x2.shape, 1)
    swapped = pltpu.roll(x2, HEAD_DIM, axis=1)
    in_lo, in_hi = (swapped, x2) if head_in_high_half else (x2, swapped)
    lo = jnp.where(lane < HEAD_DIM, in_lo, 0.0).astype(BF16)
    hi = jnp.where(lane >= HEAD_DIM, in_hi, 0.0).astype(BF16)
    return lo, hi


def _kv_pads(k, v, kh):
    col = (kh // 2) * LANES
    high = kh % 2 == 1
    return _half_padded(k[:, col:col + LANES], high) + _half_padded(v[:, col:col + LANES], high)


def _attention(items, rows, sink_at, between=lambda: None):
    keys = 2 * WINDOW
    row = lax.broadcasted_iota(jnp.int32, (2 * rows, 1), 0)
    lane = lax.broadcasted_iota(jnp.int32, (2 * rows, LANES), 1)

    scores = []
    for kh, q_at, (k_lo, k_hi, _, _), bias_at, _ in items:
        c0 = 2 * kh * LANES
        qq = jnp.concatenate([q_at(c0), q_at(c0 + LANES)], axis=0).astype(BF16)
        s = lax.dot_general(qq, jnp.concatenate([k_lo, k_hi], axis=0), (((1,), (1,)), ((), ())),
                            preferred_element_type=F32)
        scores.append([s[:, par * keys:(par + 1) * keys] + bias_at(2 * kh + par) for par in range(2)])
        between()

    sinks = [[jnp.where(row < rows, sink_at(4 * kh + par), sink_at(4 * kh + 2 + par)) for par in range(2)]
             for kh, *_ in items]
    maxes = []
    for pair, sks in zip(scores, sinks):
        maxes.append([jnp.maximum(jnp.max(s, axis=-1, keepdims=True), sk) for s, sk in zip(pair, sks)])
        between()
    probs, recips = [], []
    for pair, sks, ms in zip(scores, sinks, maxes):
        probs.append([jnp.exp(s - m) for s, m in zip(pair, ms)])
        recips.append([1.0 / (jnp.sum(p, axis=-1, keepdims=True) + jnp.exp(sk - m))
                       for p, sk, m in zip(probs[-1], sks, ms)])
        between()

    for (kh, _, (_, _, v_lo, v_hi), _, out_ref), pair, rcp in zip(items, probs, recips):
        c0 = 2 * kh * LANES
        p_cat = jnp.concatenate([p.astype(BF16) for p in pair], axis=1)
        o = jnp.dot(p_cat, jnp.concatenate([v_lo, v_hi], axis=0), preferred_element_type=F32)
        o = o * jnp.where(lane < HEAD_DIM, rcp[0], rcp[1])
        out_ref[:, c0:c0 + LANES] = o[:rows].astype(out_ref.dtype)
        out_ref[:, c0 + LANES:c0 + 2 * LANES] = o[rows:].astype(out_ref.dtype)
        between()


def _pool(rows, u_prev, u_cur, pos, wpool_ref, pscale_ref, out_ref):
    ext = jnp.concatenate([u_prev, u_cur], axis=0)
    for gi, w in enumerate(POOL_WINDOWS):
        lo = gi * POOL_GROUP_WIDTH
        e = ext[:, lo:lo + POOL_GROUP_WIDTH]
        s, d = e, 1
        while d < w:
            s = s + pltpu.roll(s, d, axis=0)
            d *= 2
        cnt = jnp.minimum(pos + 1, w).astype(F32)
        z = s[POOL_PREV_ROWS:] / cnt - e[POOL_PREV_ROWS:]
        zz = jnp.dot(z.astype(BF16), wpool_ref[gi], preferred_element_type=F32)
        zz = zz * pscale_ref[:, lo:lo + POOL_GROUP_WIDTH]
        out_ref[:, ATTN_WIDTH + lo:ATTN_WIDTH + lo + POOL_GROUP_WIDTH] = zz.astype(out_ref.dtype)


def _mixer_prompt_kernel(l, tiles_per_seq, n_tiles, sinks_ref, x_ref, xo_ref, gpre_ref, gpost_ref, win_ref, wout_ref,
                         bias_ref, wpool_ref, pscale_ref, kp_any, vp_any, pp_any,
                         out_ref, kp_ref, vp_ref, pp_ref, *scratch):
    del kp_any, vp_any, pp_any
    q_s, kv_s, u_s, mix_s = scratch[0:2], scratch[2:4], scratch[4:6], scratch[6:8]
    y_s, pads_s, utail_s = scratch[8:]
    s = pl.program_id(0)
    blocks = MIX_TM // WINDOW
    block_rows = [pl.ds(j * WINDOW, WINDOW) for j in range(blocks)]

    @pl.when(s == 0)
    def _():
        pads_s[...] = jnp.zeros_like(pads_s)
        utail_s[...] = jnp.zeros_like(utail_s)

    def step(new, project_in, attend, project_out):
        old = 1 - new
        if project_in:
            h = _rms(x_ref[...], gpre_ref[...]).astype(BF16)

        def in_projection_chunk(c):
            r = jnp.dot(h, win_ref[:, c:c + MXU_CHUNK], preferred_element_type=F32)
            if c < ATTN_WIDTH:
                q_s[new][:, c:c + MXU_CHUNK] = (r * Q_SCALE).astype(BF16)
            elif c < ATTN_WIDTH + 2 * KV_WIDTH:
                kv_s[new][:, c - ATTN_WIDTH:c - ATTN_WIDTH + MXU_CHUNK] = r
            else:
                c -= ATTN_WIDTH + 2 * KV_WIDTH
                u_s[new][:, c:c + MXU_CHUNK] = r

        def out_projection_chunk(c):
            y_s[:, c:c + MXU_CHUNK] = jnp.dot(mix_s[new][...], wout_ref[:, c:c + MXU_CHUNK],
                                              preferred_element_type=F32)

        def out_projection_finish():
            out_ref[...] = xo_ref[...] + _rms(y_s[...], gpost_ref[...])

        pending = []
        if project_out:
            pending += [functools.partial(out_projection_chunk, c) for c in range(0, D_MODEL, MXU_CHUNK)]
            pending += [out_projection_finish]
        if project_in:
            pending += [functools.partial(in_projection_chunk, c) for c in range(0, IN_WIDTH, MXU_CHUNK)]
        n_pending = len(pending)
        n_slots = 4 * N_KV_HEADS * blocks
        slots = [0]

        def between():
            slots[0] += 1
            while n_pending - len(pending) < slots[0] * n_pending // n_slots:
                pending.pop(0)()

        if not attend:
            while pending:
                pending.pop(0)()
            if project_in:
                state_outputs(new)
            return

        tile = (s - 1) % tiles_per_seq
        seq_start = tile == 0
        first = seq_start.astype(jnp.int32)
        prev_pads = [pads_s[i] for i in range(KV_PADS * N_KV_HEADS)]
        items = []
        for kh in range(N_KV_HEADS):
            pads = prev_pads[KV_PADS * kh:KV_PADS * (kh + 1)]
            for j, rows in enumerate(block_rows):
                cur = _kv_pads(kv_s[old][rows, :KV_WIDTH], kv_s[old][rows, KV_WIDTH:], kh)
                both = [jnp.concatenate([p, c], axis=0) for p, c in zip(pads, cur)]
                bias_at = (lambda i: bias_ref[first, i]) if j == 0 else (lambda i: bias_ref[0, i])
                items.append((kh, functools.partial(lambda rows, c: q_s[old][rows, c:c + LANES], rows), both,
                              bias_at, mix_s[old].at[rows]))
                pads = cur
            for i, p in enumerate(pads):
                pads_s[KV_PADS * kh + i] = p
        _attention(items, WINDOW, lambda hd: sinks_ref[l, hd], between)
        assert not pending

        u_prev = jnp.where(seq_start, 0.0, utail_s[...])
        for j, rows in enumerate(block_rows):
            u_cur = u_s[old][rows, :]
            pos = (tile * blocks + j) * WINDOW + lax.broadcasted_iota(jnp.int32, (WINDOW, 1), 0)
            _pool(WINDOW, u_prev, u_cur, pos, wpool_ref, pscale_ref, mix_s[old].at[rows])
            u_prev = u_cur[WINDOW - POOL_PREV_ROWS:]
        utail_s[...] = u_prev
        if project_in:
            state_outputs(new)

    def state_outputs(new):
        @pl.when(s % tiles_per_seq == tiles_per_seq - 1)
        def _():
            kp_ref[...] = kv_s[new][MIX_TM - WINDOW:, :KV_WIDTH]
            vp_ref[...] = kv_s[new][MIX_TM - WINDOW:, KV_WIDTH:]
            pp_ref[...] = _last_rows(u_s[new][MIX_TM - 2 * SUBLANES:, :], POOL_STATE)

    pl.when(s == 0)(functools.partial(step, 0, True, False, False))
    pl.when(s == 1)(functools.partial(step, 1, True, True, False))
    for new in range(2):
        pl.when((s >= MIX_LAG) & (s < n_tiles) & (s % 2 == new))(functools.partial(step, new, True, True, True))
    pl.when(s == n_tiles)(functools.partial(step, 0, False, True, True))
    pl.when(s == n_tiles + 1)(functools.partial(step, 1, False, False, True))


def _mixer_prompt(l, x, sinks, gpre, gpost, w_in_b, w_out_b, bias, wpool, pscale, kp, vp, pp, seq):
    m = x.shape[0]
    tm = MIX_TM
    tiles_per_seq = seq // tm
    n_tiles = m // tm
    assert n_tiles % 2 == 0 and n_tiles > MIX_LAG
    cur = lambda s: (jnp.minimum(s, n_tiles - 1), 0)
    lagging = lambda s: (jnp.maximum(s - MIX_LAG, 0), 0)
    state = lambda rows, width: pl.BlockSpec((None, None, rows, width),
                                             lambda s: (l, jnp.minimum(s, n_tiles - 1) // tiles_per_seq, 0, 0))
    return pl.pallas_call(
        functools.partial(_mixer_prompt_kernel, l, tiles_per_seq, n_tiles),
        grid=(n_tiles + MIX_LAG,),
        in_specs=[
            pl.BlockSpec(memory_space=pltpu.SMEM),
            pl.BlockSpec((tm, D_MODEL), cur),
            pl.BlockSpec((tm, D_MODEL), lagging),
            _layer_vec(l, D_MODEL),
            _layer_vec(l, D_MODEL),
            _resident((D_MODEL, IN_WIDTH), lambda s: (0, 0)),
            _resident((MIX_WIDTH, D_MODEL), lambda s: (0, 0)),
            _resident(bias.shape, lambda s: (0, 0, 0, 0)),
            _resident((None,) + wpool.shape[1:], lambda s: (l, 0, 0, 0)),
            _layer_vec(l, POOL_WIDTH),
            pl.BlockSpec(memory_space=pl.ANY),
            pl.BlockSpec(memory_space=pl.ANY),
            pl.BlockSpec(memory_space=pl.ANY),
        ],
        out_specs=[
            pl.BlockSpec((tm, D_MODEL), lagging),
            state(WINDOW, KV_WIDTH),
            state(WINDOW, KV_WIDTH),
            state(POOL_STATE, POOL_WIDTH),
        ],
        out_shape=[
            jax.ShapeDtypeStruct((m, D_MODEL), F32),
            jax.ShapeDtypeStruct(kp.shape, F32),
            jax.ShapeDtypeStruct(vp.shape, F32),
            jax.ShapeDtypeStruct(pp.shape, F32),
        ],
        scratch_shapes=[
            pltpu.VMEM((tm, ATTN_WIDTH), BF16), pltpu.VMEM((tm, ATTN_WIDTH), BF16),
            pltpu.VMEM((tm, 2 * KV_WIDTH), F32), pltpu.VMEM((tm, 2 * KV_WIDTH), F32),
            pltpu.VMEM((tm, POOL_WIDTH), F32), pltpu.VMEM((tm, POOL_WIDTH), F32),
            pltpu.VMEM((tm, MIX_WIDTH), BF16), pltpu.VMEM((tm, MIX_WIDTH), BF16),
            pltpu.VMEM((tm, D_MODEL), F32),
            pltpu.VMEM((KV_PADS * N_KV_HEADS, WINDOW, LANES), BF16),
            pltpu.VMEM((POOL_PREV_ROWS, POOL_WIDTH), F32),
        ],
        input_output_aliases={10: 1, 11: 2, 12: 3},
        compiler_params=_params("arbitrary"),
        name="mixer_prompt",
    )(sinks, x, x, gpre, gpost, w_in_b, w_out_b, bias, wpool, pscale, kp, vp, pp)


def _inproj_sample_kernel(x_ref, g_ref, w_ref, proj_ref, wb_ref, h_ref):
    j = pl.program_id(0)

    @pl.when(j == 0)
    def _():
        h_ref[...] = _rms(x_ref[...], g_ref[...]).astype(BF16)

    wb = w_ref[...].astype(BF16)
    wb_ref[...] = wb
    p = jnp.dot(h_ref[...], wb, preferred_element_type=F32)
    proj_ref[...] = p * jnp.where(j < ATTN_WIDTH // WEIGHT_BLOCK, Q_SCALE, 1.0)


def _inproj_sample(l, x, g, w):
    m = x.shape[0]
    tn = WEIGHT_BLOCK
    return pl.pallas_call(
        _inproj_sample_kernel,
        grid=(IN_WIDTH // tn,),
        in_specs=[
            pl.BlockSpec((m, D_MODEL), lambda j: (0, 0)),
            _layer_vec(l, D_MODEL),
            pl.BlockSpec((None, D_MODEL, tn), lambda j: (l, 0, j)),
        ],
        out_specs=[
            pl.BlockSpec((m, tn), lambda j: (0, j)),
            pl.BlockSpec((D_MODEL, tn), lambda j: (0, j)),
        ],
        out_shape=[
            jax.ShapeDtypeStruct((m, IN_WIDTH), F32),
            jax.ShapeDtypeStruct((D_MODEL, IN_WIDTH), BF16),
        ],
        scratch_shapes=[pltpu.VMEM((m, D_MODEL), BF16)],
        compiler_params=_params("arbitrary"),
        name="inproj_sample",
    )(x, g, w)


def _shift_in(old, new, n_new):
    r = old.shape[0]
    rolled = pltpu.roll(old, r - n_new, axis=0)
    tail = pltpu.roll(new, SUBLANES - n_new, axis=0)
    row = lax.broadcasted_iota(jnp.int32, (SUBLANES, old.shape[1]), 0)
    last = jnp.where(row < SUBLANES - n_new, rolled[r - SUBLANES:], tail)
    return jnp.concatenate([rolled[:r - SUBLANES], last], axis=0)


def _mixer_sample_kernel(l, n_new, sinks_ref, proj_ref, ck_ref, cv_ref, st_ref, bias_ref, wpool_ref, pscale_ref,
                         ks_any, vs_any, ps_any, out_ref, ko_ref, vo_ref, po_ref):
    del ks_any, vs_any, ps_any
    rows = SAMPLE_ROWS
    pad = jnp.zeros((WINDOW - rows, KV_WIDTH), F32)
    pos = PAST_LEN + lax.broadcasted_iota(jnp.int32, (rows, 1), 0)
    seqs = range(SAMPLE_GROUP)
    k_new = [proj_ref[g, :, ATTN_WIDTH:ATTN_WIDTH + KV_WIDTH] for g in seqs]
    v_new = [proj_ref[g, :, ATTN_WIDTH + KV_WIDTH:ATTN_WIDTH + 2 * KV_WIDTH] for g in seqs]
    kk = [jnp.concatenate([ck_ref[g], k_new[g], pad], axis=0) for g in seqs]
    vv = [jnp.concatenate([cv_ref[g], v_new[g], pad], axis=0) for g in seqs]
    items = [(kh, functools.partial(lambda g, c: proj_ref[g, :, c:c + LANES], g), _kv_pads(kk[g], vv[g], kh),
              lambda i: bias_ref[i], out_ref.at[g]) for kh in range(N_KV_HEADS) for g in seqs]
    _attention(items, rows, lambda hd: sinks_ref[l, hd])
    for g in seqs:
        u_prev, u_cur = st_ref[g], proj_ref[g, :, ATTN_WIDTH + 2 * KV_WIDTH:]
        _pool(rows, u_prev, u_cur, pos, wpool_ref, pscale_ref, out_ref.at[g])
        ko_ref[g] = _shift_in(ck_ref[g], k_new[g], n_new)
        vo_ref[g] = _shift_in(cv_ref[g], v_new[g], n_new)
        po_ref[g] = _last_rows(_shift_in(u_prev, u_cur, n_new), POOL_STATE)


def _mixer_sample(l, proj, cache_k, cache_v, state, sinks, bias, wpool, pscale, ks, vs, ps, n_new):
    n_seq = cache_k.shape[1]
    rows = SAMPLE_ROWS
    grp = SAMPLE_GROUP
    assert n_seq % grp == 0
    per_seq = lambda r, width: pl.BlockSpec((None, grp, r, width), lambda n: (l, n, 0, 0))
    seqs = lambda width: pl.BlockSpec((grp, rows, width), lambda n: (n, 0, 0))
    return pl.pallas_call(
        functools.partial(_mixer_sample_kernel, l, n_new),
        grid=(n_seq // grp,),
        in_specs=[
            pl.BlockSpec(memory_space=pltpu.SMEM),
            seqs(IN_WIDTH),
            per_seq(WINDOW, KV_WIDTH),
            per_seq(WINDOW, KV_WIDTH),
            per_seq(POOL_PREV_ROWS, POOL_WIDTH),
            pl.BlockSpec(bias.shape, lambda n: (0, 0, 0)),
            pl.BlockSpec((None,) + wpool.shape[1:], lambda n: (l, 0, 0, 0)),
            _layer_vec(l, POOL_WIDTH),
            pl.BlockSpec(memory_space=pl.ANY),
            pl.BlockSpec(memory_space=pl.ANY),
            pl.BlockSpec(memory_space=pl.ANY),
        ],
        out_specs=[
            seqs(MIX_WIDTH),
            per_seq(WINDOW, KV_WIDTH),
            per_seq(WINDOW, KV_WIDTH),
            per_seq(POOL_STATE, POOL_WIDTH),
        ],
        out_shape=[
            jax.ShapeDtypeStruct((n_seq, rows, MIX_WIDTH), F32),
            jax.ShapeDtypeStruct(ks.shape, F32),
            jax.ShapeDtypeStruct(vs.shape, F32),
            jax.ShapeDtypeStruct(ps.shape, F32),
        ],
        input_output_aliases={8: 1, 9: 2, 10: 3},
        compiler_params=_params("arbitrary"),
        name="mixer_sample",
    )(sinks, proj.reshape(n_seq, rows, IN_WIDTH), cache_k, cache_v, state, bias, wpool, pscale, ks, vs, ps)


def _outproj_sample_kernel(mix_ref, x_ref, g_ref, w_ref, o_ref, wb_ref):
    k = pl.program_id(0)

    @pl.when(k == 0)
    def _():
        o_ref[...] = jnp.zeros_like(o_ref)

    wb = w_ref[...].astype(BF16)
    wb_ref[...] = wb
    o_ref[...] += jnp.dot(mix_ref[...].astype(BF16), wb, preferred_element_type=F32)

    @pl.when(k == pl.num_programs(0) - 1)
    def _():
        o_ref[...] = x_ref[...] + _rms(o_ref[...], g_ref[...])


def _outproj_sample(l, mix, x, g, w):
    m = x.shape[0]
    tk = WEIGHT_BLOCK
    return pl.pallas_call(
        _outproj_sample_kernel,
        grid=(MIX_WIDTH // tk,),
        in_specs=[
            pl.BlockSpec((m, tk), lambda k: (0, k)),
            pl.BlockSpec((m, D_MODEL), lambda k: (0, 0)),
            _layer_vec(l, D_MODEL),
            pl.BlockSpec((None, tk, D_MODEL), lambda k: (l, k, 0)),
        ],
        out_specs=[
            pl.BlockSpec((m, D_MODEL), lambda k: (0, 0)),
            pl.BlockSpec((tk, D_MODEL), lambda k: (k, 0)),
        ],
        out_shape=[
            jax.ShapeDtypeStruct((m, D_MODEL), F32),
            jax.ShapeDtypeStruct((MIX_WIDTH, D_MODEL), BF16),
        ],
        compiler_params=_params("arbitrary"),
        name="outproj_sample",
    )(mix, x, g, w)


def _rms_chunked(src_ref, inv_ref, finish):
    n = src_ref.shape[0]

    def stats(c, carry):
        rows = pl.ds(pl.multiple_of(c * STAT_ROWS, STAT_ROWS), STAT_ROWS)
        y = src_ref[rows, :]
        inv = lax.rsqrt(jnp.mean(y * y, axis=-1, keepdims=True) + EPS)
        inv_ref[rows, :] = jnp.broadcast_to(inv, (STAT_ROWS, LANES))
        return carry

    def apply(c, carry):
        rows = pl.ds(pl.multiple_of(c * APPLY_ROWS, APPLY_ROWS), APPLY_ROWS)
        inv = jnp.tile(inv_ref[rows, :], (1, src_ref.shape[1] // LANES))
        finish(rows, src_ref[rows, :] * inv)
        return carry

    lax.fori_loop(0, n // STAT_ROWS, stats, 0, unroll=True)
    lax.fori_loop(0, n // APPLY_ROWS, apply, 0)


def _ffn_kernel(cast_weights, x_ref, gpre_ref, gpost_ref, wup_ref, wdn_ref, o_ref, *rest):
    f = pl.program_id(1)
    h_ref, inv_ref = rest[-2:]

    @pl.when(f == 0)
    def _():
        g = gpre_ref[...]

        def store_h(rows, xn):
            h_ref[rows, :] = (xn * g).astype(BF16)
            o_ref[rows, :] = jnp.zeros((APPLY_ROWS, D_MODEL), F32)

        _rms_chunked(x_ref, inv_ref, store_h)

    if cast_weights:
        wupb_ref, wdnb_ref = rest[:2]
        wupb_ref[...] = wup_ref[...].astype(BF16)
        wdnb_ref[...] = wdn_ref[...].astype(BF16)
        wup_ref, wdn_ref = wupb_ref, wdnb_ref
    a = jnp.dot(h_ref[...], wup_ref[...], preferred_element_type=F32)
    a = jnp.square(jnp.maximum(a, 0.0)).astype(BF16)
    o_ref[...] += jnp.dot(a, wdn_ref[...], preferred_element_type=F32)

    @pl.when(f == pl.num_programs(1) - 1)
    def _():
        g = gpost_ref[...]

        def store_out(rows, yn):
            o_ref[rows, :] = x_ref[rows, :] + yn * g

        _rms_chunked(o_ref, inv_ref, store_out)


def _ffn(l, x, gpre, gpost, wup, wdn, tm, tf, cast_weights):
    m = x.shape[0]
    if cast_weights:
        assert m == tm
        w_specs = [pl.BlockSpec((None, D_MODEL, tf), lambda i, f: (l, 0, f)),
                   pl.BlockSpec((None, tf, D_MODEL), lambda i, f: (l, f, 0))]
        extra_specs = [pl.BlockSpec((D_MODEL, tf), lambda i, f: (0, f)),
                       pl.BlockSpec((tf, D_MODEL), lambda i, f: (f, 0))]
        extra_shapes = [jax.ShapeDtypeStruct((D_MODEL, D_FF), BF16), jax.ShapeDtypeStruct((D_FF, D_MODEL), BF16)]
    else:
        w_specs = [pl.BlockSpec((D_MODEL, tf), lambda i, f: (0, f)),
                   pl.BlockSpec((tf, D_MODEL), lambda i, f: (f, 0))]
        extra_specs, extra_shapes = [], []
    return pl.pallas_call(
        functools.partial(_ffn_kernel, cast_weights),
        grid=(m // tm, D_FF // tf),
        in_specs=[
            pl.BlockSpec((tm, D_MODEL), lambda i, f: (i, 0)),
            _layer_vec(l, D_MODEL),
            _layer_vec(l, D_MODEL),
        ] + w_specs,
        out_specs=[pl.BlockSpec((tm, D_MODEL), lambda i, f: (i, 0))] + extra_specs,
        out_shape=[jax.ShapeDtypeStruct((m, D_MODEL), F32)] + extra_shapes,
        scratch_shapes=[pltpu.VMEM((tm, D_MODEL), BF16), pltpu.VMEM((tm, LANES), F32)],
        compiler_params=_params("arbitrary", "arbitrary"),
        name="ffn_sample" if cast_weights else "ffn_prompt",
    )(x, gpre, gpost, wup, wdn)


def _bias_tables():
    heads = np.arange(1, N_Q_HEADS + 1, dtype=np.float32)
    slopes = np.exp2(np.float32(-8.0) * heads / np.float32(N_Q_HEADS)).astype(np.float32)

    def table(rows, mask_prev):
        i = np.arange(rows)[:, None]
        j = np.arange(2 * WINDOW)[None, :]
        dist = i + WINDOW - j
        valid = (dist >= 0) & (dist < WINDOW)
        if mask_prev:
            valid = valid & (j >= WINDOW)
        out = np.empty((2 * N_KV_HEADS, 2 * rows, 2 * WINDOW), np.float32)
        for kh in range(N_KV_HEADS):
            for par in range(2):
                for half, head in enumerate((4 * kh + par, 4 * kh + 2 + par)):
                    bias = (-slopes[head]) * dist.astype(np.float32)
                    out[2 * kh + par, half * rows:(half + 1) * rows] = np.where(valid, bias, np.float32(NEG_BIG))
        return out

    prompt = np.stack([table(WINDOW, False), table(WINDOW, True)])
    sample = table(SAMPLE_ROWS, False)
    return jnp.asarray(prompt), jnp.asarray(sample)


def kernel(x_prompt, x_sample, cache_k, cache_v, state_pool, w_in, w_out, w_pool, pool_scale, attn_sinks,
           g_pre_mix, g_post_mix, g_pre_ffn, g_post_ffn, w_up, w_down):
    n_seq, seq, _ = x_prompt.shape
    dec_batch, dec_seq, _ = x_sample.shape
    assert seq % MIX_TM == 0 and (n_seq * seq) % FFN_TM == 0 and dec_seq <= SAMPLE_ROWS
    bias_prompt, bias_sample = _bias_tables()

    vec = lambda p: p.reshape(DEPTH, 1, p.shape[-1])
    gpm, gqm, gpf, gqf, pscale = vec(g_pre_mix), vec(g_post_mix), vec(g_pre_ffn), vec(g_post_ffn), vec(pool_scale)
    w_pool_b = w_pool.astype(BF16)

    xp = x_prompt.reshape(n_seq * seq, D_MODEL)
    xs = jnp.pad(x_sample, ((0, 0), (0, SAMPLE_ROWS - dec_seq), (0, 0))).reshape(dec_batch * SAMPLE_ROWS, D_MODEL)
    ck_all = cache_k.reshape(DEPTH, dec_batch, WINDOW, KV_WIDTH)
    cv_all = cache_v.reshape(DEPTH, dec_batch, WINDOW, KV_WIDTH)
    st_all = jnp.pad(state_pool, ((0, 0), (0, 0), (1, 0), (0, 0)))

    kp = jnp.zeros((DEPTH, n_seq, WINDOW, KV_WIDTH), F32)
    vp = jnp.zeros((DEPTH, n_seq, WINDOW, KV_WIDTH), F32)
    pp = jnp.zeros((DEPTH, n_seq, POOL_STATE, POOL_WIDTH), F32)
    ks = jnp.zeros((DEPTH, dec_batch, WINDOW, KV_WIDTH), F32)
    vs = jnp.zeros((DEPTH, dec_batch, WINDOW, KV_WIDTH), F32)
    ps = jnp.zeros((DEPTH, dec_batch, POOL_STATE, POOL_WIDTH), F32)

    for l in range(DEPTH):
        proj, w_in_b = _inproj_sample(l, xs, gpm, w_in)
        mix, ks, vs, ps = _mixer_sample(l, proj, ck_all, cv_all, st_all, attn_sinks, bias_sample, w_pool_b, pscale,
                                        ks, vs, ps, dec_seq)
        xs, w_out_b = _outproj_sample(l, mix.reshape(xs.shape[0], MIX_WIDTH), xs, gqm, w_out)
        xs, w_up_b, w_down_b = _ffn(l, xs, gpf, gqf, w_up, w_down, xs.shape[0], WEIGHT_BLOCK, True)

        xp, kp, vp, pp = _mixer_prompt(l, xp, attn_sinks, gpm, gqm, w_in_b, w_out_b, bias_prompt, w_pool_b, pscale,
                                       kp, vp, pp, seq)
        (xp,) = _ffn(l, xp, gpf, gqf, w_up_b, w_down_b, FFN_TM, FFN_TF, False)

    y_prompt = xp.reshape(n_seq, seq, D_MODEL)
    y_sample = xs.reshape(dec_batch, SAMPLE_ROWS, D_MODEL)[:, :dec_seq]
    kv_shape = (WINDOW, N_KV_HEADS, HEAD_DIM)
    return (y_prompt, y_sample, kp.reshape((DEPTH, n_seq) + kv_shape), vp.reshape((DEPTH, n_seq) + kv_shape), pp,
            ks.reshape((DEPTH, dec_batch) + kv_shape), vs.reshape((DEPTH, dec_batch) + kv_shape), ps)
```
